```python
import math
import jax, jax.numpy as jnp
from jax import lax
import numpy as np

D_MODEL = 2048
BATCH = 4
SEQ = 2048
DEPTH = 1

HEAD_DIM = 64
SWA_Q_HEADS = 16
SWA_KV_HEADS = 4
SWA_GROUP = SWA_Q_HEADS // SWA_KV_HEADS
WINDOW = 128
FOX_HEADS = 16
BLOCK = 128
N_EXPERTS = 64
TOP_K = 8
N_GROUPS = 8
TOPK_GROUPS = 4
EXPERT_FF = 512
SHARED_FF = 512
ROUTED_SCALE = 2.5
MOE_BLOCK = 128
RMS_EPS = 1e-6
FORGET_BIAS_INIT = 4.0

SWA_Q_W = SWA_Q_HEADS * HEAD_DIM
SWA_KV_W = SWA_KV_HEADS * HEAD_DIM
FOX_W = FOX_HEADS * HEAD_DIM
SPLIT_SIZES = (SWA_Q_W, SWA_KV_W, SWA_KV_W, FOX_W, FOX_W, FOX_W, FOX_HEADS, D_MODEL, D_MODEL)
SPLIT_POINTS = tuple(int(v) for v in np.cumsum(SPLIT_SIZES)[:-1])
IN_WIDTH = int(sum(SPLIT_SIZES))

kernel_name = 'hybrid_swa_fox_gated_moe_block'


def rms_norm(x, g):
    xf = x.astype(jnp.float32)
    y = xf * lax.rsqrt(jnp.mean(xf * xf, axis=-1, keepdims=True) + RMS_EPS)
    return (y * g.astype(jnp.float32)).astype(x.dtype)


def alibi_slopes(n):
    return jnp.asarray([2.0 ** (-8.0 * (h + 1) / n) for h in range(n)], jnp.float32)


def sliding_window_attention(q, k, v, sinks):
    B, S = q.shape[0], q.shape[1]
    nb = S // BLOCK
    qb = q.reshape(B, nb, BLOCK, SWA_KV_HEADS, SWA_GROUP, HEAD_DIM)
    kb = k.reshape(B, nb, BLOCK, SWA_KV_HEADS, HEAD_DIM)
    vb = v.reshape(B, nb, BLOCK, SWA_KV_HEADS, HEAD_DIM)
    pad = ((0, 0), (1, 0), (0, 0), (0, 0), (0, 0))
    kk = jnp.concatenate([jnp.pad(kb, pad)[:, :-1], kb], axis=2)
    vv = jnp.concatenate([jnp.pad(vb, pad)[:, :-1], vb], axis=2)
    s = jnp.einsum('bnqhgd,bnkhd->bnhgqk', qb, kk).astype(jnp.float32) * (HEAD_DIM ** -0.5)
    qpos = jnp.arange(BLOCK)[:, None] + BLOCK
    kpos = jnp.arange(2 * BLOCK)[None, :]
    dist = qpos - kpos
    in_band = (dist >= 0) & (dist < WINDOW)
    has_prev = (jnp.arange(nb)[:, None, None] > 0) | (kpos[None] >= BLOCK)
    valid = in_band[None] & has_prev
    slopes = alibi_slopes(SWA_Q_HEADS).reshape(SWA_KV_HEADS, SWA_GROUP)
    s = s - slopes[:, :, None, None] * dist.astype(jnp.float32)
    s = jnp.where(valid[None, :, None, None], s, -jnp.inf)
    sink = sinks.astype(jnp.float32).reshape(SWA_KV_HEADS, SWA_GROUP)[:, :, None, None]
    m = jnp.maximum(jnp.max(s, axis=-1, keepdims=True), sink)
    p = jnp.exp(s - m)
    p = p / (jnp.sum(p, axis=-1, keepdims=True) + jnp.exp(sink - m))
    o = jnp.einsum('bnhgqk,bnkhd->bnqhgd', p.astype(v.dtype), vv)
    return o.reshape(B, S, SWA_Q_W)


def forgetting_attention(q, k, v, f_logit):
    B, S = q.shape[0], q.shape[1]
    c = jnp.cumsum(jax.nn.log_sigmoid(f_logit.astype(jnp.float32)), axis=1)
    c = jnp.transpose(c, (0, 2, 1))
    scale = HEAD_DIM ** -0.5
    outs = []
    for i in range(S // BLOCK):
        q0, q1 = i * BLOCK, (i + 1) * BLOCK
        s = jnp.einsum('bqhd,bkhd->bhqk', q[:, q0:q1], k[:, :q1]).astype(jnp.float32) * scale
        s = s + c[:, :, q0:q1, None] - c[:, :, None, :q1]
        causal = jnp.arange(q1)[None, :] <= (q0 + jnp.arange(BLOCK))[:, None]
        p = jax.nn.softmax(jnp.where(causal, s, -jnp.inf), axis=-1)
        outs.append(jnp.einsum('bhqk,bkhd->bqhd', p.astype(v.dtype), v[:, :q1]))
    return jnp.concatenate(outs, axis=1).reshape(B, S, FOX_W)


def swiglu(x, wg, wu, wd):
    return (jax.nn.silu(x @ wg) * (x @ wu)) @ wd


def route(u, w_router, router_bias):
    T = u.shape[0]
    scores = jax.nn.sigmoid((u @ w_router).astype(jnp.float32))
    biased = scores + router_bias.astype(jnp.float32)
    grp = biased.reshape(T, N_GROUPS, N_EXPERTS // N_GROUPS)
    grp_score = jnp.sum(lax.top_k(grp, 2)[0], axis=-1)
    _, gidx = lax.top_k(grp_score, TOPK_GROUPS)
    gmask = jnp.sum(jax.nn.one_hot(gidx, N_GROUPS, dtype=jnp.float32), axis=1) > 0
    emask = jnp.repeat(gmask, N_EXPERTS // N_GROUPS, axis=1)
    _, eidx = lax.top_k(jnp.where(emask, biased, -jnp.inf), TOP_K)
    w = jnp.take_along_axis(scores, eidx, axis=1)
    w = w / jnp.sum(w, axis=-1, keepdims=True) * ROUTED_SCALE
    return eidx, w


def routed_experts(u, eidx, gate_w, w_gate, w_up, w_down):
    T, D = u.shape
    A = T * TOP_K
    e_flat = eidx.reshape(A)
    tok_flat = (jnp.arange(A, dtype=jnp.int32) // TOP_K)
    w_flat = gate_w.reshape(A)
    order = jnp.argsort(e_flat)
    e_s, tok_s, w_s = e_flat[order], tok_flat[order], w_flat[order]
    counts = jnp.bincount(e_flat, length=N_EXPERTS)
    padded = (counts + MOE_BLOCK - 1) // MOE_BLOCK * MOE_BLOCK
    start = jnp.cumsum(counts) - counts
    pad_end = jnp.cumsum(padded)
    pad_start = pad_end - padded
    dest = jnp.arange(A) - start[e_s] + pad_start[e_s]
    n_blocks = -(-A // MOE_BLOCK) + N_EXPERTS
    P = n_blocks * MOE_BLOCK
    row_tok = jnp.zeros((P,), jnp.int32).at[dest].set(tok_s)
    row_w = jnp.zeros((P,), u.dtype).at[dest].set(w_s.astype(u.dtype))
    blk_exp = jnp.minimum(jnp.searchsorted(pad_end, jnp.arange(n_blocks) * MOE_BLOCK, side='right'), N_EXPERTS - 1)

    def run_block(args):
        toks, e = args
        return swiglu(u[toks], w_gate[e], w_up[e], w_down[e])

    y = lax.map(run_block, (row_tok.reshape(n_blocks, MOE_BLOCK), blk_exp))
    y = y.reshape(P, D) * row_w[:, None]
    return jax.ops.segment_sum(y, row_tok, num_segments=T)


def setup_inputs(seed: int = 0) -> dict:
    key = jax.random.key(seed)
    ks = jax.random.split(key, 20)
    f32 = jnp.float32

    def nrm(k, shape, scale):
        return jax.random.normal(k, shape, f32) * scale

    return {
        'x': nrm(ks[0], (BATCH, SEQ, D_MODEL), 1.0),
        'attn_norm_g': 1.0 + nrm(ks[1], (DEPTH, D_MODEL), 0.02),
        'w_in': nrm(ks[2], (DEPTH, D_MODEL, IN_WIDTH), D_MODEL ** -0.5),
        'forget_bias': FORGET_BIAS_INIT + nrm(ks[3], (DEPTH, FOX_HEADS), 0.5),
        'sinks': nrm(ks[4], (DEPTH, SWA_Q_HEADS), 0.5),
        'w_branch_swa': nrm(ks[5], (DEPTH, SWA_Q_W, D_MODEL), SWA_Q_W ** -0.5),
        'w_branch_fox': nrm(ks[6], (DEPTH, FOX_W, D_MODEL), FOX_W ** -0.5),
        'w_out': nrm(ks[7], (DEPTH, D_MODEL, D_MODEL), D_MODEL ** -0.5),
        'ffn_norm_g': 1.0 + nrm(ks[8], (DEPTH, D_MODEL), 0.02),
        'w_router': nrm(ks[9], (DEPTH, D_MODEL, N_EXPERTS), D_MODEL ** -0.5),
        'router_bias': nrm(ks[10], (DEPTH, N_EXPERTS), 0.01),
        'w_exp_gate': nrm(ks[11], (DEPTH, N_EXPERTS, D_MODEL, EXPERT_FF), D_MODEL ** -0.5),
        'w_exp_up': nrm(ks[12], (DEPTH, N_EXPERTS, D_MODEL, EXPERT_FF), D_MODEL ** -0.5),
        'w_exp_down': nrm(ks[13], (DEPTH, N_EXPERTS, EXPERT_FF, D_MODEL), EXPERT_FF ** -0.5),
        'w_sh_gate': nrm(ks[14], (DEPTH, D_MODEL, SHARED_FF), D_MODEL ** -0.5),
        'w_sh_up': nrm(ks[15], (DEPTH, D_MODEL, SHARED_FF), D_MODEL ** -0.5),
        'w_sh_down': nrm(ks[16], (DEPTH, SHARED_FF, D_MODEL), SHARED_FF ** -0.5),
        'final_norm_g': 1.0 + nrm(ks[17], (D_MODEL,), 0.02),
    }


def reference(x, attn_norm_g, w_in, forget_bias, sinks, w_branch_swa, w_branch_fox, w_out,
              ffn_norm_g, w_router, router_bias, w_exp_gate, w_exp_up, w_exp_down,
              w_sh_gate, w_sh_up, w_sh_down, final_norm_g):
    B, S, D = x.shape
    h = x
    for l in range(DEPTH):
        u = rms_norm(h, attn_norm_g[l])
        z = u @ w_in[l]
        a_q, a_k, a_v, f_q, f_k, f_v, f_g, g_swa, g_fox = jnp.split(z, SPLIT_POINTS, axis=-1)
        y_swa = sliding_window_attention(
            a_q.reshape(B, S, SWA_Q_HEADS, HEAD_DIM),
            a_k.reshape(B, S, SWA_KV_HEADS, HEAD_DIM),
            a_v.reshape(B, S, SWA_KV_HEADS, HEAD_DIM),
            sinks[l])
        y_fox = forgetting_attention(
            f_q.reshape(B, S, FOX_HEADS, HEAD_DIM),
            f_k.reshape(B, S, FOX_HEADS, HEAD_DIM),
            f_v.reshape(B, S, FOX_HEADS, HEAD_DIM),
            f_g + forget_bias[l])
        mix = (jax.nn.sigmoid(g_swa) * (y_swa @ w_branch_swa[l])
               + jax.nn.sigmoid(g_fox) * (y_fox @ w_branch_fox[l]))
        h = h + mix @ w_out[l]
        v = rms_norm(h, ffn_norm_g[l]).reshape(B * S, D)
        eidx, gate_w = route(v, w_router[l], router_bias[l])
        routed = routed_experts(v, eidx, gate_w, w_exp_gate[l], w_exp_up[l], w_exp_down[l])
        shared = swiglu(v, w_sh_gate[l], w_sh_up[l], w_sh_down[l])
        h = h + (routed + shared).reshape(B, S, D)
    return rms_norm(h, final_norm_g)
```

```python
import functools

import numpy as np
import jax
import jax.numpy as jnp
from jax import lax
from jax.experimental import pallas as pl
from jax.experimental.pallas import tpu as pltpu

F32 = jnp.float32
BF16 = jnp.bfloat16
U32 = jnp.uint32
I32 = jnp.int32

HEAD_DIM = 64
SWA_Q_HEADS = 16
SWA_KV_HEADS = 4
FOX_HEADS = 16
WINDOW = 128
BLOCK = 128
N_GROUPS = 8
TOPK_GROUPS = 4
TOP_K = 8
ROUTED_SCALE = 2.5
RMS_EPS = 1e-6
LANES = 128
SWA_Q_W = SWA_Q_HEADS * HEAD_DIM
SWA_KV_W = SWA_KV_HEADS * HEAD_DIM
FOX_W = FOX_HEADS * HEAD_DIM
FORGET_W = FOX_HEADS
SWA_KV_EXP_W = 2 * SWA_KV_W
Z_TILE = 512
FOX_TK = 256
MOE_ROWS = 256
VMEM_LIMIT = 56 * 1024 * 1024
HI = lax.Precision.HIGHEST
NEG_INF = float("-inf")


def _cparams(n_axes):
    return pltpu.CompilerParams(dimension_semantics=("arbitrary",) * n_axes,
                                vmem_limit_bytes=VMEM_LIMIT)


def _rms(x, g):
    return x * lax.rsqrt(jnp.mean(x * x, axis=-1, keepdims=True) + RMS_EPS) * g


def _pack_pairs(v):
    half = v.shape[1] // 2
    bits = lax.bitcast_convert_type(v.astype(BF16).astype(F32), U32)
    return (bits[:, :half] >> 16) | (bits[:, half:] & jnp.uint32(0xFFFF0000))


def _unpack_lo(w):
    return lax.bitcast_convert_type(w << 16, F32)


def _unpack_hi(w):
    return lax.bitcast_convert_type(w & jnp.uint32(0xFFFF0000), F32)


def _store_slabs(ref, packed, rows):
    n = packed.shape[1] // LANES
    for j in range(n):
        ref[pl.ds(j, rows, stride=n), :] = packed[:, j * LANES:(j + 1) * LANES]


def _load_slabs(ref, base, rows, n, dtype):
    lo, hi = [], []
    for j in range(n):
        w = ref[pl.ds(base + j, rows, stride=n), :]
        lo.append(_unpack_lo(w).astype(dtype))
        hi.append(_unpack_hi(w).astype(dtype))
    return lo, hi


def _inproj_kernel(x_ref, g_ref, w_ref, wf_ref, z_ref, f_ref, u_scr, *, n_gate_tiles):
    j = pl.program_id(1)

    @pl.when(j == 0)
    def _():
        u = _rms(x_ref[...], g_ref[...]).astype(BF16)
        u_scr[...] = u
        f_ref[...] = jnp.dot(u, wf_ref[...], preferred_element_type=F32)

    z = jnp.dot(u_scr[...], w_ref[...], preferred_element_type=F32)

    @pl.when(j < n_gate_tiles)
    def _():
        z_ref[...] = jax.nn.sigmoid(z).astype(BF16)

    @pl.when(j >= n_gate_tiles)
    def _():
        z_ref[...] = z.astype(BF16)


def _inproj(x2, g, w_main, w_f, tm):
    T, D = x2.shape
    ZW = w_main.shape[1]
    return pl.pallas_call(
        functools.partial(_inproj_kernel, n_gate_tiles=2 * D // Z_TILE),
        grid=(T // tm, ZW // Z_TILE),
        in_specs=[
            pl.BlockSpec((tm, D), lambda i, j: (i, 0)),
            pl.BlockSpec((1, D), lambda i, j: (0, 0)),
            pl.BlockSpec((D, Z_TILE), lambda i, j: (0, j)),
            pl.BlockSpec((D, LANES), lambda i, j: (0, 0)),
        ],
        out_specs=[
            pl.BlockSpec((tm, Z_TILE), lambda i, j: (i, j)),
            pl.BlockSpec((tm, LANES), lambda i, j: (i, 0)),
        ],
        out_shape=[jax.ShapeDtypeStruct((T, ZW), BF16), jax.ShapeDtypeStruct((T, LANES), F32)],
        scratch_shapes=[pltpu.VMEM((tm, D), BF16)],
        compiler_params=_cparams(2),
        name="inproj",
    )(x2, g, w_main, w_f)


def _fcum_kernel(f_ref, b_ref, ccol_ref, crow_ref, carry_ref):
    @pl.when(pl.program_id(1) == 0)
    def _():
        carry_ref[...] = jnp.zeros_like(carry_ref)

    a = f_ref[...] + b_ref[...]
    ls = jnp.minimum(a, 0.0) - jnp.log1p(jnp.exp(-jnp.abs(a)))
    n = ls.shape[0]
    tri = (lax.broadcasted_iota(I32, (n, n), 1) <= lax.broadcasted_iota(I32, (n, n), 0)).astype(F32)
    c = jnp.dot(tri, ls, precision=HI, preferred_element_type=F32) + carry_ref[...]
    ccol_ref[...] = c
    crow_ref[0, 0] = c.T[:FORGET_W, :]
    carry_ref[...] = c[n - 1:n, :]


def _forget_cumsum(f, bias, B, S):
    T = B * S
    nb = S // FOX_TK
    return pl.pallas_call(
        _fcum_kernel,
        grid=(B, nb),
        in_specs=[
            pl.BlockSpec((FOX_TK, LANES), lambda b, j: (b * nb + j, 0)),
            pl.BlockSpec((1, LANES), lambda b, j: (0, 0)),
        ],
        out_specs=[
            pl.BlockSpec((FOX_TK, LANES), lambda b, j: (b * nb + j, 0)),
            pl.BlockSpec((1, 1, FORGET_W, FOX_TK), lambda b, j: (b, j, 0, 0)),
        ],
        out_shape=[jax.ShapeDtypeStruct((T, LANES), F32),
                   jax.ShapeDtypeStruct((B, nb, FORGET_W, FOX_TK), F32)],
        scratch_shapes=[pltpu.VMEM((1, LANES), F32)],
        compiler_params=_cparams(2),
        name="forget_cumsum",
    )(f, bias)


def _swa_kernel(sink_ref, q_ref, kp_ref, kc_ref, vp_ref, vc_ref, y_ref, *, slopes):
    qi = pl.program_id(1)
    kk = jnp.concatenate([kp_ref[...], kc_ref[...]], axis=0)
    vv = jnp.concatenate([vp_ref[...], vc_ref[...]], axis=0)
    row = lax.broadcasted_iota(I32, (BLOCK, 2 * BLOCK), 0)
    col = lax.broadcasted_iota(I32, (BLOCK, 2 * BLOCK), 1)
    dist = row + BLOCK - col
    valid = (dist >= 0) & (dist < WINDOW) & ((qi > 0) | (col >= BLOCK))
    distf = dist.astype(F32)
    lo_half = lax.broadcasted_iota(I32, (BLOCK, LANES), 1) < HEAD_DIM
    group = SWA_Q_HEADS // SWA_KV_HEADS
    for pair in range(SWA_Q_HEADS // 2):
        q2 = q_ref[:, pair * LANES:(pair + 1) * LANES]
        g = (2 * pair) // group
        kg = kk[:, g * LANES:(g + 1) * LANES]
        vg = vv[:, g * LANES:(g + 1) * LANES]
        outs = []
        for half in range(2):
            h = 2 * pair + half
            keep = lo_half if half == 0 else jnp.logical_not(lo_half)
            qm = jnp.where(keep, q2, jnp.zeros_like(q2)) * jnp.asarray(HEAD_DIM ** -0.5, BF16)
            s = lax.dot_general(qm, kg, (((1,), (1,)), ((), ())), preferred_element_type=F32)
            s = jnp.where(valid, s - slopes[h] * distf, NEG_INF)
            sink = sink_ref[h]
            m = jnp.maximum(jnp.max(s, axis=-1, keepdims=True), sink)
            p = jnp.exp(s - m)
            denom = jnp.sum(p, axis=-1, keepdims=True) + jnp.exp(sink - m)
            o = jnp.dot(p.astype(BF16), vg, preferred_element_type=F32)
            outs.append(o / denom)
        y_ref[:, pair * LANES:(pair + 1) * LANES] = jnp.where(lo_half, outs[0], outs[1]).astype(BF16)


def _swa(z, sinks, B, S, col0):
    T = B * S
    nq = S // BLOCK
    qb = col0 // SWA_Q_W
    kb = (col0 + SWA_Q_W) // SWA_KV_EXP_W
    vb = kb + 1
    slopes = tuple(2.0 ** (-8.0 * (h + 1) / SWA_Q_HEADS) for h in range(SWA_Q_HEADS))
    cur = lambda b, i: b * nq + i
    prev = lambda b, i: jnp.maximum(b * nq + i - 1, 0)
    return pl.pallas_call(
        functools.partial(_swa_kernel, slopes=slopes),
        grid=(B, nq),
        in_specs=[
            pl.BlockSpec(memory_space=pltpu.SMEM),
            pl.BlockSpec((BLOCK, SWA_Q_W), lambda b, i: (cur(b, i), qb)),
            pl.BlockSpec((BLOCK, SWA_KV_EXP_W), lambda b, i: (prev(b, i), kb)),
            pl.BlockSpec((BLOCK, SWA_KV_EXP_W), lambda b, i: (cur(b, i), kb)),
            pl.BlockSpec((BLOCK, SWA_KV_EXP_W), lambda b, i: (prev(b, i), vb)),
            pl.BlockSpec((BLOCK, SWA_KV_EXP_W), lambda b, i: (cur(b, i), vb)),
        ],
        out_specs=pl.BlockSpec((BLOCK, SWA_Q_W), lambda b, i: (cur(b, i), 0)),
        out_shape=jax.ShapeDtypeStruct((T, SWA_Q_W), BF16),
        compiler_params=_cparams(2),
        name="swa",
    )(sinks, z, z, z, z, z)


def _fox_kernel(q_ref, k_ref, v_ref, ccol_ref, crow_ref, y_ref):
    qi = pl.program_id(1)
    n_full = qi // (FOX_TK // BLOCK)
    lo_half = lax.broadcasted_iota(I32, (BLOCK, LANES), 1) < HEAD_DIM
    row = qi * BLOCK + lax.broadcasted_iota(I32, (BLOCK, FOX_TK), 0)
    col = lax.broadcasted_iota(I32, (BLOCK, FOX_TK), 1)

    for pair in range(FOX_HEADS // 2):
        lanes = slice(pair * LANES, (pair + 1) * LANES)
        q2 = q_ref[:, lanes]
        outs = []
        for half in range(2):
            h = 2 * pair + half
            keep = lo_half if half == 0 else jnp.logical_not(lo_half)
            qm = jnp.where(keep, q2, jnp.zeros_like(q2)) * jnp.asarray(HEAD_DIM ** -0.5, BF16)
            ct = ccol_ref[:, h:h + 1]

            def step(j, carry, masked, qm=qm, ct=ct, h=h, lanes=lanes):
                m, l, acc = carry
                k0 = pl.multiple_of(j * FOX_TK, FOX_TK)
                kb = k_ref[pl.ds(k0, FOX_TK), lanes]
                vb = v_ref[pl.ds(k0, FOX_TK), lanes]
                s = lax.dot_general(qm, kb, (((1,), (1,)), ((), ())), preferred_element_type=F32)
                s = s + (ct - crow_ref[0, j, h:h + 1, :])
                if masked:
                    s = jnp.where(col + j * FOX_TK <= row, s, NEG_INF)
                m_new = jnp.maximum(m, jnp.max(s, axis=-1, keepdims=True))
                alpha = jnp.exp(m - m_new)
                p = jnp.exp(s - m_new)
                l = alpha * l + jnp.sum(p, axis=-1, keepdims=True)
                acc = alpha * acc + jnp.dot(p.astype(BF16), vb, preferred_element_type=F32)
                return m_new, l, acc

            init = (jnp.full((BLOCK, 1), -1e30, F32), jnp.zeros((BLOCK, 1), F32),
                    jnp.zeros((BLOCK, LANES), F32))
            carry = lax.fori_loop(0, n_full, functools.partial(step, masked=False), init)
            m, l, acc = step(n_full, carry, True)
            outs.append(acc / l)
        y_ref[:, lanes] = jnp.where(lo_half, outs[0], outs[1]).astype(BF16)


def _fox(z, ccol, crow, B, S, col0):
    T = B * S
    nq = S // BLOCK
    nb = S // FOX_TK
    qb = col0 // FOX_W
    return pl.pallas_call(
        _fox_kernel,
        grid=(B, nq),
        in_specs=[
            pl.BlockSpec((BLOCK, FOX_W), lambda b, i: (b * nq + i, qb)),
            pl.BlockSpec((S, FOX_W), lambda b, i: (b, qb + 1)),
            pl.BlockSpec((S, FOX_W), lambda b, i: (b, qb + 2)),
            pl.BlockSpec((BLOCK, LANES), lambda b, i: (b * nq + i, 0)),
            pl.BlockSpec((1, nb, FORGET_W, FOX_TK), lambda b, i: (b, 0, 0, 0)),
        ],
        out_specs=pl.BlockSpec((BLOCK, FOX_W), lambda b, i: (b * nq + i, 0)),
        out_shape=jax.ShapeDtypeStruct((T, FOX_W), BF16),
        compiler_params=_cparams(2),
        name="fox",
    )(z, z, z, ccol, crow)


def _first_argmax(vals, iota, n):
    m = jnp.max(vals, axis=0, keepdims=True)
    idx = jnp.min(jnp.where(vals == m, iota, n), axis=0, keepdims=True)
    return m, idx


def _route(logits_t, rb):
    E, tm = logits_t.shape
    gs = E // N_GROUPS
    scores = jax.nn.sigmoid(logits_t)
    biased = scores + rb
    iota_g = lax.broadcasted_iota(I32, (gs, tm), 0)
    iota_n = lax.broadcasted_iota(I32, (N_GROUPS, tm), 0)
    grp = jnp.zeros((N_GROUPS, tm), F32)
    for g in range(N_GROUPS):
        blk = biased[g * gs:(g + 1) * gs, :]
        m1, i1 = _first_argmax(blk, iota_g, gs)
        m2 = jnp.max(jnp.where(iota_g == i1, NEG_INF, blk), axis=0, keepdims=True)
        grp = jnp.where(iota_n == g, m1 + m2, grp)
    gsel = jnp.zeros((N_GROUPS, tm), jnp.bool_)
    cur = grp
    for _ in range(TOPK_GROUPS):
        _, gi = _first_argmax(cur, iota_n, N_GROUPS)
        oh = iota_n == gi
        gsel = gsel | oh
        cur = jnp.where(oh, NEG_INF, cur)
    gself = gsel.astype(F32)
    emask = jnp.concatenate(
        [jnp.broadcast_to(gself[g:g + 1, :], (gs, tm)) for g in range(N_GROUPS)], axis=0) > 0.5
    cur = jnp.where(emask, biased, NEG_INF)
    iota_e = lax.broadcasted_iota(I32, (E, tm), 0)
    sel = jnp.zeros((E, tm), jnp.bool_)
    picks = []
    for _ in range(TOP_K):
        _, ei = _first_argmax(cur, iota_e, E)
        oh = iota_e == ei
        sel = sel | oh
        cur = jnp.where(oh, NEG_INF, cur)
        picks.append(ei)
    w = jnp.where(sel, scores, 0.0)
    w = w / jnp.sum(w, axis=0, keepdims=True) * ROUTED_SCALE
    return sel, w, picks, iota_e


def _post_kernel(ysw_ref, yfx_ref, gsw_ref, gfx_ref, x_ref, wsw_ref, wfx_ref, wo_ref, g2_ref,
                 wrt_ref, rb_ref, h1_ref, vp_ref, eidx_ref, rank_ref, wgt_ref, cnt_ref, carry_ref):
    @pl.when(pl.program_id(0) == 0)
    def _():
        carry_ref[...] = jnp.zeros_like(carry_ref)

    tm = x_ref.shape[0]
    a = jnp.dot(ysw_ref[...], wsw_ref[...], preferred_element_type=F32)
    b = jnp.dot(yfx_ref[...], wfx_ref[...], preferred_element_type=F32)
    mix = gsw_ref[...].astype(F32) * a + gfx_ref[...].astype(F32) * b
    h1 = x_ref[...] + jnp.dot(mix.astype(BF16), wo_ref[...], preferred_element_type=F32)
    h1_ref[...] = h1
    v = _rms(h1, g2_ref[...])
    _store_slabs(vp_ref, _pack_pairs(v), tm)

    logits_t = lax.dot_general(wrt_ref[...], v, (((1,), (1,)), ((), ())),
                               precision=HI, preferred_element_type=F32)
    sel, w, picks, iota_e = _route(logits_t, rb_ref[...])

    before = (lax.broadcasted_iota(I32, (tm, tm), 0) < lax.broadcasted_iota(I32, (tm, tm), 1)).astype(BF16)
    rank = jnp.dot(sel.astype(BF16), before, preferred_element_type=F32) + carry_ref[...]
    carry_ref[...] = carry_ref[...] + jnp.sum(sel.astype(F32), axis=1, keepdims=True)
    cnt_ref[...] = carry_ref[...]

    iota_k = lax.broadcasted_iota(I32, (TOP_K, tm), 0)
    eidx = jnp.zeros((TOP_K, tm), I32)
    rnk = jnp.zeros((TOP_K, tm), F32)
    wgt = jnp.zeros((TOP_K, tm), F32)
    for k, ei in enumerate(picks):
        oh = iota_e == ei
        eidx = jnp.where(iota_k == k, ei, eidx)
        rnk = jnp.where(iota_k == k, jnp.sum(jnp.where(oh, rank, 0.0), axis=0, keepdims=True), rnk)
        wgt = jnp.where(iota_k == k, jnp.sum(jnp.where(oh, w, 0.0), axis=0, keepdims=True), wgt)
    eidx_ref[...] = eidx
    rank_ref[...] = rnk.astype(I32)
    wgt_ref[...] = wgt


def _post(ysw, yfx, z, x2, wsw, wfx, wo, g2, wrt, rb, tm):
    T, D = x2.shape
    E = wrt.shape[0]
    nch = D // 2 // LANES
    const = lambda i: (0, 0)
    return pl.pallas_call(
        _post_kernel,
        grid=(T // tm,),
        in_specs=[
            pl.BlockSpec((tm, SWA_Q_W), lambda i: (i, 0)),
            pl.BlockSpec((tm, FOX_W), lambda i: (i, 0)),
            pl.BlockSpec((tm, D), lambda i: (i, 0)),
            pl.BlockSpec((tm, D), lambda i: (i, 1)),
            pl.BlockSpec((tm, D), lambda i: (i, 0)),
            pl.BlockSpec((SWA_Q_W, D), const),
            pl.BlockSpec((FOX_W, D), const),
            pl.BlockSpec((D, D), const),
            pl.BlockSpec((1, D), const),
            pl.BlockSpec((E, D), const),
            pl.BlockSpec((E, 1), const),
        ],
        out_specs=[
            pl.BlockSpec((tm, D), lambda i: (i, 0)),
            pl.BlockSpec((tm * nch, LANES), lambda i: (i, 0)),
            pl.BlockSpec((TOP_K, tm), lambda i: (0, i)),
            pl.BlockSpec((TOP_K, tm), lambda i: (0, i)),
            pl.BlockSpec((TOP_K, tm), lambda i: (0, i)),
            pl.BlockSpec((E, 1), const),
        ],
        out_shape=[
            jax.ShapeDtypeStruct((T, D), F32),
            jax.ShapeDtypeStruct((T * nch, LANES), U32),
            jax.ShapeDtypeStruct((TOP_K, T), I32),
            jax.ShapeDtypeStruct((TOP_K, T), I32),
            jax.ShapeDtypeStruct((TOP_K, T), F32),
            jax.ShapeDtypeStruct((E, 1), F32),
        ],
        scratch_shapes=[pltpu.VMEM((E, 1), F32)],
        compiler_params=_cparams(1),
        name="post_mixer",
    )(ysw, yfx, z, z, x2, wsw, wfx, wo, g2, wrt, rb)


def _dispatch_kernel(dest_ref, vp_hbm, xs_in, xs_out, sem, *, td, nch):
    del xs_in
    base = pl.program_id(0) * td

    def row_copy(src_row, dst_row):
        return pltpu.make_async_copy(vp_hbm.at[pl.ds(src_row, nch), :],
                                     xs_out.at[pl.ds(dst_row, nch), :], sem)

    def issue(t, c):
        src = pl.multiple_of((base + t) * nch, nch)
        for k in range(TOP_K):
            row_copy(src, pl.multiple_of(dest_ref[t * TOP_K + k] * nch, nch)).start()
        return c

    lax.fori_loop(0, td, issue, 0)

    def drain(t, c):
        for k in range(TOP_K):
            row_copy(0, 0).wait()
        return c

    lax.fori_loop(0, td, drain, 0)


def _dispatch(dest_flat, vp, n_rows, td):
    nch = vp.shape[0] * TOP_K // dest_flat.shape[0]
    T = vp.shape[0] // nch
    xs0 = jnp.zeros((n_rows * nch, LANES), U32)
    return pl.pallas_call(
        functools.partial(_dispatch_kernel, td=td, nch=nch),
        grid=(T // td,),
        in_specs=[
            pl.BlockSpec((td * TOP_K,), lambda i: (i,), memory_space=pltpu.SMEM),
            pl.BlockSpec(memory_space=pl.ANY),
            pl.BlockSpec(memory_space=pl.ANY),
        ],
        out_specs=pl.BlockSpec(memory_space=pl.ANY),
        out_shape=jax.ShapeDtypeStruct((n_rows * nch, LANES), U32),
        scratch_shapes=[pltpu.SemaphoreType.DMA(())],
        input_output_aliases={2: 0},
        compiler_params=_cparams(1),
        name="dispatch",
    )(dest_flat, vp, xs0)


def _experts_kernel(be_ref, nu_ref, x_ref, wg_ref, wu_ref, wd_ref, y_ref, wg_s, wu_s, wd_s, *, rows, nch):
    b = pl.program_id(0)

    @pl.when(b < nu_ref[0])
    def _():
        @pl.when((b == 0) | (be_ref[b] != be_ref[jnp.maximum(b - 1, 0)]))
        def _():
            wg_s[...] = wg_ref[0].astype(BF16)
            wu_s[...] = wu_ref[0].astype(BF16)
            wd_s[...] = wd_ref[0].astype(BF16)

        lo, hi = _load_slabs(x_ref, 0, rows, nch, BF16)
        x = jnp.concatenate(lo + hi, axis=1)
        hg = jnp.dot(x, wg_s[...], preferred_element_type=F32)
        hu = jnp.dot(x, wu_s[...], preferred_element_type=F32)
        act = (jax.nn.silu(hg) * hu).astype(BF16)
        y = jnp.dot(act, wd_s[...], preferred_element_type=F32)
        _store_slabs(y_ref, _pack_pairs(y), rows)

    @pl.when(b >= nu_ref[0])
    def _():
        y_ref[...] = jnp.zeros_like(y_ref)


def _experts(blk_exp, n_used, xs, wg, wu, wd, rows):
    E, D, FF = wg.shape
    nch = D // 2 // LANES
    nblk = xs.shape[0] // (rows * nch)
    xmap = lambda b, be, nu: (jnp.minimum(b, nu[0] - 1), 0)
    wmap = lambda b, be, nu: (be[b], 0, 0)
    return pl.pallas_call(
        functools.partial(_experts_kernel, rows=rows, nch=nch),
        grid_spec=pltpu.PrefetchScalarGridSpec(
            num_scalar_prefetch=2,
            grid=(nblk,),
            in_specs=[
                pl.BlockSpec((rows * nch, LANES), xmap),
                pl.BlockSpec((1, D, FF), wmap),
                pl.BlockSpec((1, D, FF), wmap),
                pl.BlockSpec((1, FF, D), wmap),
            ],
            out_specs=pl.BlockSpec((rows * nch, LANES), lambda b, be, nu: (b, 0)),
            scratch_shapes=[pltpu.VMEM((D, FF), BF16), pltpu.VMEM((D, FF), BF16), pltpu.VMEM((FF, D), BF16)],
        ),
        out_shape=jax.ShapeDtypeStruct(xs.shape, U32),
        compiler_params=_cparams(1),
        name="experts",
    )(blk_exp, n_used, xs, wg, wu, wd)


def _combine_kernel(dest_ref, wtok_ref, h1_ref, vp_ref, wsg_ref, wsu_ref, wsd_ref, gf_ref, ys_hbm,
                    out_ref, buf, sem, *, tc, nch):
    def row_copy(src_row, slot):
        return pltpu.make_async_copy(ys_hbm.at[pl.ds(src_row, nch), :],
                                     buf.at[pl.ds(slot, nch), :], sem)

    def issue(t, c):
        for k in range(TOP_K):
            row_copy(pl.multiple_of(dest_ref[t * TOP_K + k] * nch, nch),
                     pl.multiple_of((k * tc + t) * nch, nch)).start()
        return c

    lax.fori_loop(0, tc, issue, 0)

    lo, hi = _load_slabs(vp_ref, 0, tc, nch, BF16)
    x = jnp.concatenate(lo + hi, axis=1)
    hg = jnp.dot(x, wsg_ref[...], preferred_element_type=F32)
    hu = jnp.dot(x, wsu_ref[...], preferred_element_type=F32)
    act = (jax.nn.silu(hg) * hu).astype(BF16)
    h = h1_ref[...] + jnp.dot(act, wsd_ref[...], preferred_element_type=F32)

    def drain(t, c):
        for k in range(TOP_K):
            row_copy(0, 0).wait()
        return c

    lax.fori_loop(0, tc, drain, 0)

    lo_acc = [jnp.zeros((tc, LANES), F32) for _ in range(nch)]
    hi_acc = [jnp.zeros((tc, LANES), F32) for _ in range(nch)]
    for k in range(TOP_K):
        wk = wtok_ref[:, k:k + 1]
        lo, hi = _load_slabs(buf, k * tc * nch, tc, nch, F32)
        for j in range(nch):
            lo_acc[j] = lo_acc[j] + wk * lo[j]
            hi_acc[j] = hi_acc[j] + wk * hi[j]
    h = h + jnp.concatenate(lo_acc + hi_acc, axis=1)
    out_ref[...] = _rms(h, gf_ref[...])


def _combine(dest_flat, wtok, h1, vp, wsg, wsu, wsd, gf, ys, tc):
    T, D = h1.shape
    FF = wsg.shape[1]
    nch = D // 2 // LANES
    const = lambda i: (0, 0)
    return pl.pallas_call(
        functools.partial(_combine_kernel, tc=tc, nch=nch),
        grid=(T // tc,),
        in_specs=[
            pl.BlockSpec((tc * TOP_K,), lambda i: (i,), memory_space=pltpu.SMEM),
            pl.BlockSpec((tc, TOP_K), lambda i: (i, 0)),
            pl.BlockSpec((tc, D), lambda i: (i, 0)),
            pl.BlockSpec((tc * nch, LANES), lambda i: (i, 0)),
            pl.BlockSpec((D, FF), const),
            pl.BlockSpec((D, FF), const),
            pl.BlockSpec((FF, D), const),
            pl.BlockSpec((1, D), const),
            pl.BlockSpec(memory_space=pl.ANY),
        ],
        out_specs=pl.BlockSpec((tc, D), lambda i: (i, 0)),
        out_shape=jax.ShapeDtypeStruct((T, D), F32),
        scratch_shapes=[pltpu.VMEM((TOP_K * tc * nch, LANES), U32), pltpu.SemaphoreType.DMA(())],
        compiler_params=_cparams(1),
        name="combine",
    )(dest_flat, wtok, h1, vp, wsg, wsu, wsd, gf, ys)


def _layer(h, attn_norm_g, w_in, forget_bias, sinks, w_branch_swa, w_branch_fox, w_out, ffn_norm_g,
           w_router, router_bias, w_exp_gate, w_exp_up, w_exp_down, w_sh_gate, w_sh_up, w_sh_down, out_g):
    B, S, D = h.shape
    T = B * S
    E = w_router.shape[1]
    x2 = h.reshape(T, D)

    o = np.cumsum([0, SWA_Q_W, SWA_KV_W, SWA_KV_W, FOX_W, FOX_W, FOX_W, FORGET_W, D, D])
    aq, ak, av, fq, fk, fv, fg, gsw, gfx = [w_in[:, o[i]:o[i + 1]] for i in range(9)]

    def dup(w):
        w4 = w.reshape(D, SWA_KV_HEADS, 1, HEAD_DIM)
        return jnp.broadcast_to(w4, (D, SWA_KV_HEADS, 2, HEAD_DIM)).reshape(D, SWA_KV_EXP_W)

    w_main = jnp.concatenate([gsw, gfx, aq, dup(ak), dup(av), fq, fk, fv], axis=1).astype(BF16)
    w_f = jnp.pad(fg, ((0, 0), (0, LANES - FORGET_W))).astype(BF16)
    fb = jnp.pad(forget_bias.astype(F32), (0, LANES - FORGET_W)).reshape(1, LANES)

    z, f = _inproj(x2, attn_norm_g.reshape(1, D), w_main, w_f, tm=min(1024, T))
    ccol, crow = _forget_cumsum(f, fb, B, S)
    y_swa = _swa(z, sinks.astype(F32), B, S, col0=2 * D)
    y_fox = _fox(z, ccol, crow, B, S, col0=2 * D + SWA_Q_W + 2 * SWA_KV_EXP_W)

    h1, vp, eidx, rank, wgt, cnt = _post(
        y_swa, y_fox, z, x2, w_branch_swa.astype(BF16), w_branch_fox.astype(BF16), w_out.astype(BF16),
        ffn_norm_g.reshape(1, D), w_router.T, router_bias.reshape(E, 1).astype(F32), tm=min(256, T))

    rows = MOE_ROWS
    n_blocks = T * TOP_K // rows + E
    counts = cnt[:, 0].astype(I32)
    padded = (counts + rows - 1) // rows * rows
    pad_end = jnp.cumsum(padded)
    pad_start = pad_end - padded
    dest_flat = (pad_start[eidx] + rank).T.reshape(T * TOP_K)
    n_used = (pad_end[-1] // rows).astype(I32)
    blk = jnp.arange(n_blocks, dtype=I32)
    blk_exp = jnp.minimum(jnp.searchsorted(pad_end, blk * rows, side="right"), E - 1).astype(I32)
    blk_exp = jnp.where(blk < n_used, blk_exp, blk_exp[n_used - 1])

    xs = _dispatch(dest_flat, vp, n_blocks * rows, td=min(256, T))
    ys = _experts(blk_exp, n_used.reshape(1), xs, w_exp_gate, w_exp_up, w_exp_down, rows)
    out = _combine(dest_flat, wgt.T, h1, vp, w_sh_gate.astype(BF16), w_sh_up.astype(BF16),
                   w_sh_down.astype(BF16), out_g.reshape(1, D), ys, tc=min(256, T))
    return out.reshape(B, S, D)


def kernel(x, attn_norm_g, w_in, forget_bias, sinks, w_branch_swa, w_branch_fox, w_out, ffn_norm_g,
           w_router, router_bias, w_exp_gate, w_exp_up, w_exp_down, w_sh_gate, w_sh_up, w_sh_down,
           final_norm_g):
    depth = w_in.shape[0]
    assert depth == 1, "the final norm is fused into the last layer's combine"
    return _layer(x, attn_norm_g[0], w_in[0], forget_bias[0], sinks[0], w_branch_swa[0], w_branch_fox[0],
                  w_out[0], ffn_norm_g[0], w_router[0], router_bias[0], w_exp_gate[0], w_exp_up[0],
                  w_exp_down[0], w_sh_gate[0], w_sh_up[0], w_sh_down[0], final_norm_g)
```

```python
import functools

import numpy as np
import jax
import jax.numpy as jnp
from jax import lax
from jax.experimental import pallas as pl
from jax.experimental.pallas import tpu as pltpu

F32 = jnp.float32
BF16 = jnp.bfloat16
U32 = jnp.uint32
I32 = jnp.int32

HEAD_DIM = 64
SWA_Q_HEADS = 16
SWA_KV_HEADS = 4
FOX_HEADS = 16
WINDOW = 128
BLOCK = 128
N_GROUPS = 8
TOPK_GROUPS = 4
TOP_K = 8
ROUTED_SCALE = 2.5
RMS_EPS = 1e-6
LANES = 128
SWA_Q_W = SWA_Q_HEADS * HEAD_DIM
SWA_KV_W = SWA_KV_HEADS * HEAD_DIM
FOX_W = FOX_HEADS * HEAD_DIM
FORGET_W = FOX_HEADS
SWA_KV_EXP_W = 2 * SWA_KV_W
Z_TILE = 512
FOX_TK = 256
FOX_PAIRS_PER_LOOP = 2
MOE_ROWS = 256
VMEM_LIMIT = 56 * 1024 * 1024
HI = lax.Precision.HIGHEST
NEG_INF = float("-inf")


def _cparams(n_axes):
    return pltpu.CompilerParams(dimension_semantics=("arbitrary",) * n_axes,
                                vmem_limit_bytes=VMEM_LIMIT)


def _rms(x, g):
    return x * lax.rsqrt(jnp.mean(x * x, axis=-1, keepdims=True) + RMS_EPS) * g


def _pack_pairs(v):
    half = v.shape[1] // 2
    bits = lax.bitcast_convert_type(v.astype(BF16).astype(F32), U32)
    return (bits[:, :half] >> 16) | (bits[:, half:] & jnp.uint32(0xFFFF0000))


def _unpack_lo(w):
    return lax.bitcast_convert_type(w << 16, F32)


def _unpack_hi(w):
    return lax.bitcast_convert_type(w & jnp.uint32(0xFFFF0000), F32)


def _store_slabs(ref, packed, rows):
    n = packed.shape[1] // LANES
    for j in range(n):
        ref[pl.ds(j, rows, stride=n), :] = packed[:, j * LANES:(j + 1) * LANES]


def _load_slabs(ref, base, rows, n, dtype):
    lo, hi = [], []
    for j in range(n):
        w = ref[pl.ds(base + j, rows, stride=n), :]
        lo.append(_unpack_lo(w).astype(dtype))
        hi.append(_unpack_hi(w).astype(dtype))
    return lo, hi


def _inproj_kernel(x_ref, g_ref, w_ref, wf_ref, z_ref, f_ref, u_scr, *, n_gate_tiles):
    j = pl.program_id(1)

    @pl.when(j == 0)
    def _():
        u = _rms(x_ref[...], g_ref[...]).astype(BF16)
        u_scr[...] = u
        f_ref[...] = jnp.dot(u, wf_ref[...], preferred_element_type=F32)

    z = jnp.dot(u_scr[...], w_ref[...], preferred_element_type=F32)

    @pl.when(j < n_gate_tiles)
    def _():
        z_ref[...] = jax.nn.sigmoid(z).astype(BF16)

    @pl.when(j >= n_gate_tiles)
    def _():
        z_ref[...] = z.astype(BF16)


def _inproj(x2, g, w_main, w_f, tm):
    T, D = x2.shape
    ZW = w_main.shape[1]
    return pl.pallas_call(
        functools.partial(_inproj_kernel, n_gate_tiles=2 * D // Z_TILE),
        grid=(T // tm, ZW // Z_TILE),
        in_specs=[
            pl.BlockSpec((tm, D), lambda i, j: (i, 0)),
            pl.BlockSpec((1, D), lambda i, j: (0, 0)),
            pl.BlockSpec((D, Z_TILE), lambda i, j: (0, j)),
            pl.BlockSpec((D, LANES), lambda i, j: (0, 0)),
        ],
        out_specs=[
            pl.BlockSpec((tm, Z_TILE), lambda i, j: (i, j)),
            pl.BlockSpec((tm, LANES), lambda i, j: (i, 0)),
        ],
        out_shape=[jax.ShapeDtypeStruct((T, ZW), BF16), jax.ShapeDtypeStruct((T, LANES), F32)],
        scratch_shapes=[pltpu.VMEM((tm, D), BF16)],
        compiler_params=_cparams(2),
        name="inproj",
    )(x2, g, w_main, w_f)


def _fcum_kernel(f_ref, b_ref, ccol_ref, crow_ref, carry_ref):
    @pl.when(pl.program_id(1) == 0)
    def _():
        carry_ref[...] = jnp.zeros_like(carry_ref)

    a = f_ref[...] + b_ref[...]
    ls = jnp.minimum(a, 0.0) - jnp.log1p(jnp.exp(-jnp.abs(a)))
    n = ls.shape[0]
    tri = (lax.broadcasted_iota(I32, (n, n), 1) <= lax.broadcasted_iota(I32, (n, n), 0)).astype(F32)
    c = jnp.dot(tri, ls, precision=HI, preferred_element_type=F32) + carry_ref[...]
    ccol_ref[...] = c
    crow_ref[0, 0] = c.T[:FORGET_W, :]
    carry_ref[...] = c[n - 1:n, :]


def _forget_cumsum(f, bias, B, S):
    T = B * S
    nb = S // FOX_TK
    return pl.pallas_call(
        _fcum_kernel,
        grid=(B, nb),
        in_specs=[
            pl.BlockSpec((FOX_TK, LANES), lambda b, j: (b * nb + j, 0)),
            pl.BlockSpec((1, LANES), lambda b, j: (0, 0)),
        ],
        out_specs=[
            pl.BlockSpec((FOX_TK, LANES), lambda b, j: (b * nb + j, 0)),
            pl.BlockSpec((1, 1, FORGET_W, FOX_TK), lambda b, j: (b, j, 0, 0)),
        ],
        out_shape=[jax.ShapeDtypeStruct((T, LANES), F32),
                   jax.ShapeDtypeStruct((B, nb, FORGET_W, FOX_TK), F32)],
        scratch_shapes=[pltpu.VMEM((1, LANES), F32)],
        compiler_params=_cparams(2),
        name="forget_cumsum",
    )(f, bias)


def _swa_kernel(sink_ref, q_ref, kp_ref, kc_ref, vp_ref, vc_ref, y_ref, *, slopes):
    qi = pl.program_id(1)
    kk = jnp.concatenate([kp_ref[...], kc_ref[...]], axis=0)
    vv = jnp.concatenate([vp_ref[...], vc_ref[...]], axis=0)
    row = lax.broadcasted_iota(I32, (BLOCK, 2 * BLOCK), 0)
    col = lax.broadcasted_iota(I32, (BLOCK, 2 * BLOCK), 1)
    dist = row + BLOCK - col
    valid = (dist >= 0) & (dist < WINDOW) & ((qi > 0) | (col >= BLOCK))
    distf = dist.astype(F32)
    lo_half = lax.broadcasted_iota(I32, (BLOCK, LANES), 1) < HEAD_DIM
    group = SWA_Q_HEADS // SWA_KV_HEADS
    for pair in range(SWA_Q_HEADS // 2):
        q2 = q_ref[:, pair * LANES:(pair + 1) * LANES]
        g = (2 * pair) // group
        kg = kk[:, g * LANES:(g + 1) * LANES]
        vg = vv[:, g * LANES:(g + 1) * LANES]
        outs = []
        for half in range(2):
            h = 2 * pair + half
            keep = lo_half if half == 0 else jnp.logical_not(lo_half)
            qm = jnp.where(keep, q2, jnp.zeros_like(q2)) * jnp.asarray(HEAD_DIM ** -0.5, BF16)
            s = lax.dot_general(qm, kg, (((1,), (1,)), ((), ())), preferred_element_type=F32)
            s = jnp.where(valid, s - slopes[h] * distf, NEG_INF)
            sink = sink_ref[h]
            m = jnp.maximum(jnp.max(s, axis=-1, keepdims=True), sink)
            p = jnp.exp(s - m)
            denom = jnp.sum(p, axis=-1, keepdims=True) + jnp.exp(sink - m)
            o = jnp.dot(p.astype(BF16), vg, preferred_element_type=F32)
            outs.append(o / denom)
        y_ref[:, pair * LANES:(pair + 1) * LANES] = jnp.where(lo_half, outs[0], outs[1]).astype(BF16)


def _swa(z, sinks, B, S, col0):
    T = B * S
    nq = S // BLOCK
    qb = col0 // SWA_Q_W
    kb = (col0 + SWA_Q_W) // SWA_KV_EXP_W
    vb = kb + 1
    slopes = tuple(2.0 ** (-8.0 * (h + 1) / SWA_Q_HEADS) for h in range(SWA_Q_HEADS))
    cur = lambda b, i: b * nq + i
    prev = lambda b, i: jnp.maximum(b * nq + i - 1, 0)
    return pl.pallas_call(
        functools.partial(_swa_kernel, slopes=slopes),
        grid=(B, nq),
        in_specs=[
            pl.BlockSpec(memory_space=pltpu.SMEM),
            pl.BlockSpec((BLOCK, SWA_Q_W), lambda b, i: (cur(b, i), qb)),
            pl.BlockSpec((BLOCK, SWA_KV_EXP_W), lambda b, i: (prev(b, i), kb)),
            pl.BlockSpec((BLOCK, SWA_KV_EXP_W), lambda b, i: (cur(b, i), kb)),
            pl.BlockSpec((BLOCK, SWA_KV_EXP_W), lambda b, i: (prev(b, i), vb)),
            pl.BlockSpec((BLOCK, SWA_KV_EXP_W), lambda b, i: (cur(b, i), vb)),
        ],
        out_specs=pl.BlockSpec((BLOCK, SWA_Q_W), lambda b, i: (cur(b, i), 0)),
        out_shape=jax.ShapeDtypeStruct((T, SWA_Q_W), BF16),
        compiler_params=_cparams(2),
        name="swa",
    )(sinks, z, z, z, z, z)


def _fox_kernel(q_ref, k_ref, v_ref, ccol_ref, crow_ref, y_ref):
    qi = pl.program_id(1)
    n_full = qi // (FOX_TK // BLOCK)
    lo_half = lax.broadcasted_iota(I32, (BLOCK, LANES), 1) < HEAD_DIM
    row = qi * BLOCK + lax.broadcasted_iota(I32, (BLOCK, FOX_TK), 0)
    col = lax.broadcasted_iota(I32, (BLOCK, FOX_TK), 1)

    for first_pair in range(0, FOX_HEADS // 2, FOX_PAIRS_PER_LOOP):
        pairs = list(range(first_pair, first_pair + FOX_PAIRS_PER_LOOP))
        qms, cts = [], []
        for pair in pairs:
            q2 = q_ref[:, pair * LANES:(pair + 1) * LANES]
            for half in range(2):
                keep = lo_half if half == 0 else jnp.logical_not(lo_half)
                qms.append(jnp.where(keep, q2, jnp.zeros_like(q2)) * jnp.asarray(HEAD_DIM ** -0.5, BF16))
                cts.append(ccol_ref[:, 2 * pair + half:2 * pair + half + 1])

        def step(j, carry, masked, pairs=pairs, qms=qms, cts=cts):
            k0 = pl.multiple_of(j * FOX_TK, FOX_TK)
            out = []
            for n, pair in enumerate(pairs):
                lanes = slice(pair * LANES, (pair + 1) * LANES)
                kb = k_ref[pl.ds(k0, FOX_TK), lanes]
                vb = v_ref[pl.ds(k0, FOX_TK), lanes]
                for half in range(2):
                    i = 2 * n + half
                    h = 2 * pair + half
                    m, l, acc = carry[3 * i:3 * i + 3]
                    s = lax.dot_general(qms[i], kb, (((1,), (1,)), ((), ())), preferred_element_type=F32)
                    s = s + (cts[i] - crow_ref[0, j, h:h + 1, :])
                    if masked:
                        s = jnp.where(col + j * FOX_TK <= row, s, NEG_INF)
                    m_new = jnp.maximum(m, jnp.max(s, axis=-1, keepdims=True))
                    alpha = jnp.exp(m - m_new)
                    p = jnp.exp(s - m_new)
                    l = alpha * l + jnp.sum(p, axis=-1, keepdims=True)
                    acc = alpha * acc + jnp.dot(p.astype(BF16), vb, preferred_element_type=F32)
                    out += [m_new, l, acc]
            return tuple(out)

        init = (jnp.full((BLOCK, 1), -1e30, F32), jnp.zeros((BLOCK, 1), F32),
                jnp.zeros((BLOCK, LANES), F32)) * (2 * FOX_PAIRS_PER_LOOP)
        carry = lax.fori_loop(0, n_full, functools.partial(step, masked=False), init)
        carry = step(n_full, carry, True)
        for n, pair in enumerate(pairs):
            o_lo = carry[6 * n + 2] / carry[6 * n + 1]
            o_hi = carry[6 * n + 5] / carry[6 * n + 4]
            y_ref[:, pair * LANES:(pair + 1) * LANES] = jnp.where(lo_half, o_lo, o_hi).astype(BF16)


def _fox(z, ccol, crow, B, S, col0):
    T = B * S
    nq = S // BLOCK
    nb = S // FOX_TK
    qb = col0 // FOX_W
    return pl.pallas_call(
        _fox_kernel,
        grid=(B, nq),
        in_specs=[
            pl.BlockSpec((BLOCK, FOX_W), lambda b, i: (b * nq + i, qb)),
            pl.BlockSpec((S, FOX_W), lambda b, i: (b, qb + 1)),
            pl.BlockSpec((S, FOX_W), lambda b, i: (b, qb + 2)),
            pl.BlockSpec((BLOCK, LANES), lambda b, i: (b * nq + i, 0)),
            pl.BlockSpec((1, nb, FORGET_W, FOX_TK), lambda b, i: (b, 0, 0, 0)),
        ],
        out_specs=pl.BlockSpec((BLOCK, FOX_W), lambda b, i: (b * nq + i, 0)),
        out_shape=jax.ShapeDtypeStruct((T, FOX_W), BF16),
        compiler_params=_cparams(2),
        name="fox",
    )(z, z, z, ccol, crow)


def _first_argmax(vals, iota, n):
    m = jnp.max(vals, axis=0, keepdims=True)
    idx = jnp.min(jnp.where(vals == m, iota, n), axis=0, keepdims=True)
    return m, idx


def _route(logits_t, rb):
    E, tm = logits_t.shape
    gs = E // N_GROUPS
    scores = jax.nn.sigmoid(logits_t)
    biased = scores + rb
    iota_g = lax.broadcasted_iota(I32, (gs, tm), 0)
    iota_n = lax.broadcasted_iota(I32, (N_GROUPS, tm), 0)
    grp = jnp.zeros((N_GROUPS, tm), F32)
    for g in range(N_GROUPS):
        blk = biased[g * gs:(g + 1) * gs, :]
        m1, i1 = _first_argmax(blk, iota_g, gs)
        m2 = jnp.max(jnp.where(iota_g == i1, NEG_INF, blk), axis=0, keepdims=True)
        grp = jnp.where(iota_n == g, m1 + m2, grp)
    gsel = jnp.zeros((N_GROUPS, tm), jnp.bool_)
    cur = grp
    for _ in range(TOPK_GROUPS):
        _, gi = _first_argmax(cur, iota_n, N_GROUPS)
        oh = iota_n == gi
        gsel = gsel | oh
        cur = jnp.where(oh, NEG_INF, cur)
    gself = gsel.astype(F32)
    emask = jnp.concatenate(
        [jnp.broadcast_to(gself[g:g + 1, :], (gs, tm)) for g in range(N_GROUPS)], axis=0) > 0.5
    cur = jnp.where(emask, biased, NEG_INF)
    iota_e = lax.broadcasted_iota(I32, (E, tm), 0)
    sel = jnp.zeros((E, tm), jnp.bool_)
    picks = []
    for _ in range(TOP_K):
        _, ei = _first_argmax(cur, iota_e, E)
        oh = iota_e == ei
        sel = sel | oh
        cur = jnp.where(oh, NEG_INF, cur)
        picks.append(ei)
    w = jnp.where(sel, scores, 0.0)
    w = w / jnp.sum(w, axis=0, keepdims=True) * ROUTED_SCALE
    return sel, w, picks, iota_e


def _post_kernel(ysw_ref, yfx_ref, gsw_ref, gfx_ref, x_ref, wsw_ref, wfx_ref, wo_ref, g2_ref,
                 wrt_ref, rb_ref, h1_ref, vp_ref, eidx_ref, rank_ref, wgt_ref, cnt_ref, carry_ref):
    @pl.when(pl.program_id(0) == 0)
    def _():
        carry_ref[...] = jnp.zeros_like(carry_ref)

    tm = x_ref.shape[0]
    a = jnp.dot(ysw_ref[...], wsw_ref[...], preferred_element_type=F32)
    b = jnp.dot(yfx_ref[...], wfx_ref[...], preferred_element_type=F32)
    mix = gsw_ref[...].astype(F32) * a + gfx_ref[...].astype(F32) * b
    h1 = x_ref[...] + jnp.dot(mix.astype(BF16), wo_ref[...], preferred_element_type=F32)
    h1_ref[...] = h1
    v = _rms(h1, g2_ref[...])
    _store_slabs(vp_ref, _pack_pairs(v), tm)

    logits_t = lax.dot_general(wrt_ref[...], v, (((1,), (1,)), ((), ())),
                               precision=HI, preferred_element_type=F32)
    sel, w, picks, iota_e = _route(logits_t, rb_ref[...])

    before = (lax.broadcasted_iota(I32, (tm, tm), 0) < lax.broadcasted_iota(I32, (tm, tm), 1)).astype(BF16)
    rank = jnp.dot(sel.astype(BF16), before, preferred_element_type=F32) + carry_ref[...]
    carry_ref[...] = carry_ref[...] + jnp.sum(sel.astype(F32), axis=1, keepdims=True)
    cnt_ref[...] = carry_ref[...]

    iota_k = lax.broadcasted_iota(I32, (TOP_K, tm), 0)
    eidx = jnp.zeros((TOP_K, tm), I32)
    rnk = jnp.zeros((TOP_K, tm), F32)
    wgt = jnp.zeros((TOP_K, tm), F32)
    for k, ei in enumerate(picks):
        oh = iota_e == ei
        eidx = jnp.where(iota_k == k, ei, eidx)
        rnk = jnp.where(iota_k == k, jnp.sum(jnp.where(oh, rank, 0.0), axis=0, keepdims=True), rnk)
        wgt = jnp.where(iota_k == k, jnp.sum(jnp.where(oh, w, 0.0), axis=0, keepdims=True), wgt)
    eidx_ref[...] = eidx
    rank_ref[...] = rnk.astype(I32)
    wgt_ref[...] = wgt


def _post(ysw, yfx, z, x2, wsw, wfx, wo, g2, wrt, rb, tm):
    T, D = x2.shape
    E = wrt.shape[0]
    nch = D // 2 // LANES
    const = lambda i: (0, 0)
    return pl.pallas_call(
        _post_kernel,
        grid=(T // tm,),
        in_specs=[
            pl.BlockSpec((tm, SWA_Q_W), lambda i: (i, 0)),
            pl.BlockSpec((tm, FOX_W), lambda i: (i, 0)),
            pl.BlockSpec((tm, D), lambda i: (i, 0)),
            pl.BlockSpec((tm, D), lambda i: (i, 1)),
            pl.BlockSpec((tm, D), lambda i: (i, 0)),
            pl.BlockSpec((SWA_Q_W, D), const),
            pl.BlockSpec((FOX_W, D), const),
            pl.BlockSpec((D, D), const),
            pl.BlockSpec((1, D), const),
            pl.BlockSpec((E, D), const),
            pl.BlockSpec((E, 1), const),
        ],
        out_specs=[
            pl.BlockSpec((tm, D), lambda i: (i, 0)),
            pl.BlockSpec((tm * nch, LANES), lambda i: (i, 0)),
            pl.BlockSpec((TOP_K, tm), lambda i: (0, i)),
            pl.BlockSpec((TOP_K, tm), lambda i: (0, i)),
            pl.BlockSpec((TOP_K, tm), lambda i: (0, i)),
            pl.BlockSpec((E, 1), const),
        ],
        out_shape=[
            jax.ShapeDtypeStruct((T, D), F32),
            jax.ShapeDtypeStruct((T * nch, LANES), U32),
            jax.ShapeDtypeStruct((TOP_K, T), I32),
            jax.ShapeDtypeStruct((TOP_K, T), I32),
            jax.ShapeDtypeStruct((TOP_K, T), F32),
            jax.ShapeDtypeStruct((E, 1), F32),
        ],
        scratch_shapes=[pltpu.VMEM((E, 1), F32)],
        compiler_params=_cparams(1),
        name="post_mixer",
    )(ysw, yfx, z, z, x2, wsw, wfx, wo, g2, wrt, rb)


def _dispatch_kernel(dest_ref, vp_ref, xs_in, xs_out, sem, *, td, nch):
    del xs_in

    def row_copy(src_row, dst_row):
        return pltpu.make_async_copy(vp_ref.at[pl.ds(src_row, nch), :],
                                     xs_out.at[pl.ds(dst_row, nch), :], sem)

    def issue(t, c):
        src = pl.multiple_of(t * nch, nch)
        for k in range(TOP_K):
            row_copy(src, pl.multiple_of(dest_ref[t * TOP_K + k] * nch, nch)).start()
        return c

    lax.fori_loop(0, td, issue, 0)

    def drain(t, c):
        for k in range(TOP_K):
            row_copy(0, 0).wait()
        return c

    lax.fori_loop(0, td, drain, 0)


def _dispatch(dest_flat, vp, n_rows, td):
    nch = vp.shape[0] * TOP_K // dest_flat.shape[0]
    T = vp.shape[0] // nch
    xs0 = jnp.zeros((n_rows * nch, LANES), U32)
    return pl.pallas_call(
        functools.partial(_dispatch_kernel, td=td, nch=nch),
        grid=(T // td,),
        in_specs=[
            pl.BlockSpec((td * TOP_K,), lambda i: (i,), memory_space=pltpu.SMEM),
            pl.BlockSpec((td * nch, LANES), lambda i: (i, 0)),
            pl.BlockSpec(memory_space=pl.ANY),
        ],
        out_specs=pl.BlockSpec(memory_space=pl.ANY),
        out_shape=jax.ShapeDtypeStruct((n_rows * nch, LANES), U32),
        scratch_shapes=[pltpu.SemaphoreType.DMA(())],
        input_output_aliases={2: 0},
        compiler_params=_cparams(1),
        name="dispatch",
    )(dest_flat, vp, xs0)


def _experts_kernel(be_ref, nu_ref, x_ref, wg_ref, wu_ref, wd_ref, y_ref, wg_s, wu_s, wd_s, *, rows, nch):
    b = pl.program_id(0)

    @pl.when(b < nu_ref[0])
    def _():
        @pl.when((b == 0) | (be_ref[b] != be_ref[jnp.maximum(b - 1, 0)]))
        def _():
            wg_s[...] = wg_ref[0].astype(BF16)
            wu_s[...] = wu_ref[0].astype(BF16)
            wd_s[...] = wd_ref[0].astype(BF16)

        lo, hi = _load_slabs(x_ref, 0, rows, nch, BF16)
        x = jnp.concatenate(lo + hi, axis=1)
        hg = jnp.dot(x, wg_s[...], preferred_element_type=F32)
        hu = jnp.dot(x, wu_s[...], preferred_element_type=F32)
        act = (jax.nn.silu(hg) * hu).astype(BF16)
        y = jnp.dot(act, wd_s[...], preferred_element_type=F32)
        _store_slabs(y_ref, _pack_pairs(y), rows)

    @pl.when(b >= nu_ref[0])
    def _():
        y_ref[...] = jnp.zeros_like(y_ref)


def _experts(blk_exp, n_used, xs, wg, wu, wd, rows):
    E, D, FF = wg.shape
    nch = D // 2 // LANES
    nblk = xs.shape[0] // (rows * nch)
    xmap = lambda b, be, nu: (jnp.minimum(b, nu[0] - 1), 0)
    wmap = lambda b, be, nu: (be[b], 0, 0)
    return pl.pallas_call(
        functools.partial(_experts_kernel, rows=rows, nch=nch),
        grid_spec=pltpu.PrefetchScalarGridSpec(
            num_scalar_prefetch=2,
            grid=(nblk,),
            in_specs=[
                pl.BlockSpec((rows * nch, LANES), xmap),
                pl.BlockSpec((1, D, FF), wmap),
                pl.BlockSpec((1, D, FF), wmap),
                pl.BlockSpec((1, FF, D), wmap),
            ],
            out_specs=pl.BlockSpec((rows * nch, LANES), lambda b, be, nu: (b, 0)),
            scratch_shapes=[pltpu.VMEM((D, FF), BF16), pltpu.VMEM((D, FF), BF16), pltpu.VMEM((FF, D), BF16)],
        ),
        out_shape=jax.ShapeDtypeStruct(xs.shape, U32),
        compiler_params=_cparams(1),
        name="experts",
    )(blk_exp, n_used, xs, wg, wu, wd)


def _combine_kernel(dest_ref, wtok_ref, h1_ref, vp_ref, wsg_ref, wsu_ref, wsd_ref, gf_ref, ys_hbm,
                    out_ref, buf, sem, *, tc, nch):
    def row_copy(src_row, slot):
        return pltpu.make_async_copy(ys_hbm.at[pl.ds(src_row, nch), :],
                                     buf.at[pl.ds(slot, nch), :], sem)

    def issue(t, c):
        for k in range(TOP_K):
            row_copy(pl.multiple_of(dest_ref[t * TOP_K + k] * nch, nch),
                     pl.multiple_of((k * tc + t) * nch, nch)).start()
        return c

    lax.fori_loop(0, tc, issue, 0)

    lo, hi = _load_slabs(vp_ref, 0, tc, nch, BF16)
    x = jnp.concatenate(lo + hi, axis=1)
    hg = jnp.dot(x, wsg_ref[...], preferred_element_type=F32)
    hu = jnp.dot(x, wsu_ref[...], preferred_element_type=F32)
    act = (jax.nn.silu(hg) * hu).astype(BF16)
    h = h1_ref[...] + jnp.dot(act, wsd_ref[...], preferred_element_type=F32)

    def drain(t, c):
        for k in range(TOP_K):
            row_copy(0, 0).wait()
        return c

    lax.fori_loop(0, tc, drain, 0)

    lo_acc = [jnp.zeros((tc, LANES), F32) for _ in range(nch)]
    hi_acc = [jnp.zeros((tc, LANES), F32) for _ in range(nch)]
    for k in range(TOP_K):
        wk = wtok_ref[:, k:k + 1]
        lo, hi = _load_slabs(buf, k * tc * nch, tc, nch, F32)
        for j in range(nch):
            lo_acc[j] = lo_acc[j] + wk * lo[j]
            hi_acc[j] = hi_acc[j] + wk * hi[j]
    h = h + jnp.concatenate(lo_acc + hi_acc, axis=1)
    out_ref[...] = _rms(h, gf_ref[...])


def _combine(dest_flat, wtok, h1, vp, wsg, wsu, wsd, gf, ys, tc):
    T, D = h1.shape
    FF = wsg.shape[1]
    nch = D // 2 // LANES
    const = lambda i: (0, 0)
    return pl.pallas_call(
        functools.partial(_combine_kernel, tc=tc, nch=nch),
        grid=(T // tc,),
        in_specs=[
            pl.BlockSpec((tc * TOP_K,), lambda i: (i,), memory_space=pltpu.SMEM),
            pl.BlockSpec((tc, TOP_K), lambda i: (i, 0)),
            pl.BlockSpec((tc, D), lambda i: (i, 0)),
            pl.BlockSpec((tc * nch, LANES), lambda i: (i, 0)),
            pl.BlockSpec((D, FF), const),
            pl.BlockSpec((D, FF), const),
            pl.BlockSpec((FF, D), const),
            pl.BlockSpec((1, D), const),
            pl.BlockSpec(memory_space=pl.ANY),
        ],
        out_specs=pl.BlockSpec((tc, D), lambda i: (i, 0)),
        out_shape=jax.ShapeDtypeStruct((T, D), F32),
        scratch_shapes=[pltpu.VMEM((TOP_K * tc * nch, LANES), U32), pltpu.SemaphoreType.DMA(())],
        compiler_params=_cparams(1),
        name="combine",
    )(dest_flat, wtok, h1, vp, wsg, wsu, wsd, gf, ys)


def _layer(h, attn_norm_g, w_in, forget_bias, sinks, w_branch_swa, w_branch_fox, w_out, ffn_norm_g,
           w_router, router_bias, w_exp_gate, w_exp_up, w_exp_down, w_sh_gate, w_sh_up, w_sh_down, out_g):
    B, S, D = h.shape
    T = B * S
    E = w_router.shape[1]
    x2 = h.reshape(T, D)

    o = np.cumsum([0, SWA_Q_W, SWA_KV_W, SWA_KV_W, FOX_W, FOX_W, FOX_W, FORGET_W, D, D])
    aq, ak, av, fq, fk, fv, fg, gsw, gfx = [w_in[:, o[i]:o[i + 1]] for i in range(9)]

    def dup(w):
        w4 = w.reshape(D, SWA_KV_HEADS, 1, HEAD_DIM)
        return jnp.broadcast_to(w4, (D, SWA_KV_HEADS, 2, HEAD_DIM)).reshape(D, SWA_KV_EXP_W)

    w_main = jnp.concatenate([gsw, gfx, aq, dup(ak), dup(av), fq, fk, fv], axis=1).astype(BF16)
    w_f = jnp.pad(fg, ((0, 0), (0, LANES - FORGET_W))).astype(BF16)
    fb = jnp.pad(forget_bias.astype(F32), (0, LANES - FORGET_W)).reshape(1, LANES)

    z, f = _inproj(x2, attn_norm_g.reshape(1, D), w_main, w_f, tm=min(1024, T))
    ccol, crow = _forget_cumsum(f, fb, B, S)
    y_swa = _swa(z, sinks.astype(F32), B, S, col0=2 * D)
    y_fox = _fox(z, ccol, crow, B, S, col0=2 * D + SWA_Q_W + 2 * SWA_KV_EXP_W)

    h1, vp, eidx, rank, wgt, cnt = _post(
        y_swa, y_fox, z, x2, w_branch_swa.astype(BF16), w_branch_fox.astype(BF16), w_out.astype(BF16),
        ffn_norm_g.reshape(1, D), w_router.T, router_bias.reshape(E, 1).astype(F32), tm=min(256, T))

    rows = MOE_ROWS
    n_blocks = T * TOP_K // rows + E
    counts = cnt[:, 0].astype(I32)
    padded = (counts + rows - 1) // rows * rows
    pad_end = jnp.cumsum(padded)
    pad_start = pad_end - padded
    experts = jnp.arange(E, dtype=I32)[:, None, None]
    start_of = jnp.sum(jnp.where(eidx[None] == experts, pad_start[:, None, None], 0), axis=0)
    dest_flat = (start_of + rank).T.reshape(T * TOP_K)
    n_used = (pad_end[-1] // rows).astype(I32)
    blk = jnp.arange(n_blocks, dtype=I32)
    blk_exp = jnp.minimum(jnp.sum((pad_end[None, :] <= blk[:, None] * rows).astype(I32), axis=1), E - 1)
    blk_exp = jnp.where(blk < n_used, blk_exp, blk_exp[n_used - 1])

    xs = _dispatch(dest_flat, vp, n_blocks * rows, td=min(256, T))
    ys = _experts(blk_exp, n_used.reshape(1), xs, w_exp_gate, w_exp_up, w_exp_down, rows)
    out = _combine(dest_flat, wgt.T, h1, vp, w_sh_gate.astype(BF16), w_sh_up.astype(BF16),
                   w_sh_down.astype(BF16), out_g.reshape(1, D), ys, tc=min(256, T))
    return out.reshape(B, S, D)


def kernel(x, attn_norm_g, w_in, forget_bias, sinks, w_branch_swa, w_branch_fox, w_out, ffn_norm_g,
           w_router, router_bias, w_exp_gate, w_exp_up, w_exp_down, w_sh_gate, w_sh_up, w_sh_down,
           final_norm_g):
    depth = w_in.shape[0]
    assert depth == 1, "the final norm is fused into the last layer's combine"
    return _layer(x, attn_norm_g[0], w_in[0], forget_bias[0], sinks[0], w_branch_swa[0], w_branch_fox[0],
                  w_out[0], ffn_norm_g[0], w_router[0], router_bias[0], w_exp_gate[0], w_exp_up[0],
                  w_exp_down[0], w_sh_gate[0], w_sh_up[0], w_sh_down[0], final_norm_g)
```

```python
import functools

import numpy as np
import jax
import jax.numpy as jnp
from jax import lax
from jax.experimental import pallas as pl
from jax.experimental.pallas import tpu as pltpu

F32 = jnp.float32
BF16 = jnp.bfloat16
I32 = jnp.int32

HEAD_DIM = 64
SWA_Q_HEADS = 16
SWA_KV_HEADS = 4
FOX_HEADS = 16
WINDOW = 128
BLOCK = 128
N_GROUPS = 8
TOPK_GROUPS = 4
TOP_K = 8
ROUTED_SCALE = 2.5
RMS_EPS = 1e-6
LANES = 128
SWA_Q_W = SWA_Q_HEADS * HEAD_DIM
SWA_KV_W = SWA_KV_HEADS * HEAD_DIM
FOX_W = FOX_HEADS * HEAD_DIM
FORGET_W = FOX_HEADS
FORGET_STRIDE = 8
SWA_KV_EXP_W = 2 * SWA_KV_W
Z_TILE = 512
FOX_TK = 256
FOX_PAIRS_PER_LOOP = 8
MOE_ROWS = 256
VMEM_LIMIT = 56 * 1024 * 1024
HI = lax.Precision.HIGHEST
NEG_INF = float("-inf")


def _cparams(n_axes):
    return pltpu.CompilerParams(dimension_semantics=("arbitrary",) * n_axes,
                                vmem_limit_bytes=VMEM_LIMIT)


def _rms(x, g):
    return x * lax.rsqrt(jnp.mean(x * x, axis=-1, keepdims=True) + RMS_EPS) * g


def _store_slabs(ref, scr, x):
    rows = x.shape[0]
    n = x.shape[1] // LANES
    hn = n // 2
    for j in range(n):
        scr[j // hn, pl.ds(j % hn, rows, stride=hn), :] = x[:, j * LANES:(j + 1) * LANES]
    halves = [scr[i].reshape(rows, 1, hn, LANES) for i in range(2)]
    ref[...] = jnp.concatenate(halves, axis=1).reshape(rows * n, LANES).astype(BF16)


def _fill_slab_scratch(ref, base, scr, rows, n):
    hn = n // 2
    xf = ref[pl.ds(base, rows * n), :].astype(F32).reshape(rows, 2, hn, LANES)
    for i in range(2):
        scr[i] = xf[:, i].reshape(rows * hn, LANES)


def _slab_chunk(scr, j, rows, n):
    hn = n // 2
    return scr[j // hn, pl.ds(j % hn, rows, stride=hn), :]


def _load_slabs(ref, base, scr, rows, n):
    _fill_slab_scratch(ref, base, scr, rows, n)
    return [_slab_chunk(scr, j, rows, n) for j in range(n)]


def _inproj_kernel(x_ref, g_ref, w_ref, wf_ref, z_ref, f_ref, u_scr, *, n_gate_tiles):
    j = pl.program_id(1)

    @pl.when(j == 0)
    def _():
        u = _rms(x_ref[...], g_ref[...]).astype(BF16)
        u_scr[...] = u
        f_ref[...] = jnp.dot(u, wf_ref[...], preferred_element_type=F32)

    z = jnp.dot(u_scr[...], w_ref[...], preferred_element_type=F32)

    @pl.when(j < n_gate_tiles)
    def _():
        z_ref[...] = jax.nn.sigmoid(z).astype(BF16)

    @pl.when(j >= n_gate_tiles)
    def _():
        z_ref[...] = z.astype(BF16)


def _inproj(x2, g, w_main, w_f, tm):
    T, D = x2.shape
    ZW = w_main.shape[1]
    return pl.pallas_call(
        functools.partial(_inproj_kernel, n_gate_tiles=2 * D // Z_TILE),
        grid=(T // tm, ZW // Z_TILE),
        in_specs=[
            pl.BlockSpec((tm, D), lambda i, j: (i, 0)),
            pl.BlockSpec((1, D), lambda i, j: (0, 0)),
            pl.BlockSpec((D, Z_TILE), lambda i, j: (0, j)),
            pl.BlockSpec((D, LANES), lambda i, j: (0, 0)),
        ],
        out_specs=[
            pl.BlockSpec((tm, Z_TILE), lambda i, j: (i, j)),
            pl.BlockSpec((tm, LANES), lambda i, j: (i, 0)),
        ],
        out_shape=[jax.ShapeDtypeStruct((T, ZW), BF16), jax.ShapeDtypeStruct((T, LANES), F32)],
        scratch_shapes=[pltpu.VMEM((tm, D), BF16)],
        compiler_params=_cparams(2),
        name="inproj",
    )(x2, g, w_main, w_f)


def _fcum_kernel(f_ref, b_ref, ccol_ref, carry_ref):
    @pl.when(pl.program_id(1) == 0)
    def _():
        carry_ref[...] = jnp.zeros_like(carry_ref)

    a = f_ref[...] + b_ref[...]
    ls = jnp.minimum(a, 0.0) - jnp.log1p(jnp.exp(-jnp.abs(a)))
    n = ls.shape[0]
    tri = (lax.broadcasted_iota(I32, (n, n), 1) <= lax.broadcasted_iota(I32, (n, n), 0)).astype(F32)
    c = jnp.dot(tri, ls, precision=HI, preferred_element_type=F32) + carry_ref[...]
    ccol_ref[...] = c
    carry_ref[...] = c[n - 1:n, :]


def _forget_cumsum(f, bias, B, S):
    T = B * S
    nb = S // FOX_TK
    return pl.pallas_call(
        _fcum_kernel,
        grid=(B, nb),
        in_specs=[
            pl.BlockSpec((FOX_TK, LANES), lambda b, j: (b * nb + j, 0)),
            pl.BlockSpec((1, LANES), lambda b, j: (0, 0)),
        ],
        out_specs=pl.BlockSpec((FOX_TK, LANES), lambda b, j: (b * nb + j, 0)),
        out_shape=jax.ShapeDtypeStruct((T, LANES), F32),
        scratch_shapes=[pltpu.VMEM((1, LANES), F32)],
        compiler_params=_cparams(2),
        name="forget_cumsum",
    )(f, bias)


def _swa_kernel(sink_ref, q_ref, kp_ref, kc_ref, vp_ref, vc_ref, y_ref, *, slopes):
    qi = pl.program_id(1)
    kk = jnp.concatenate([kp_ref[...], kc_ref[...]], axis=0)
    vv = jnp.concatenate([vp_ref[...], vc_ref[...]], axis=0)
    row = lax.broadcasted_iota(I32, (BLOCK, 2 * BLOCK), 0)
    col = lax.broadcasted_iota(I32, (BLOCK, 2 * BLOCK), 1)
    dist = row + BLOCK - col
    valid = (dist >= 0) & (dist < WINDOW) & ((qi > 0) | (col >= BLOCK))
    distf = dist.astype(F32)
    lo_half = lax.broadcasted_iota(I32, (BLOCK, LANES), 1) < HEAD_DIM
    group = SWA_Q_HEADS // SWA_KV_HEADS
    for pair in range(SWA_Q_HEADS // 2):
        q2 = q_ref[:, pair * LANES:(pair + 1) * LANES]
        g = (2 * pair) // group
        kg = kk[:, g * LANES:(g + 1) * LANES]
        vg = vv[:, g * LANES:(g + 1) * LANES]
        outs = []
        for half in range(2):
            h = 2 * pair + half
            keep = lo_half if half == 0 else jnp.logical_not(lo_half)
            qm = jnp.where(keep, q2, jnp.zeros_like(q2)) * jnp.asarray(HEAD_DIM ** -0.5, BF16)
            s = lax.dot_general(qm, kg, (((1,), (1,)), ((), ())), preferred_element_type=F32)
            s = jnp.where(valid, s - slopes[h] * distf, NEG_INF)
            sink = sink_ref[h]
            m = jnp.maximum(jnp.max(s, axis=-1, keepdims=True), sink)
            p = jnp.exp(s - m)
            denom = jnp.sum(p, axis=-1, keepdims=True) + jnp.exp(sink - m)
            o = jnp.dot(p.astype(BF16), vg, preferred_element_type=F32)
            outs.append(o / denom)
        y_ref[:, pair * LANES:(pair + 1) * LANES] = jnp.where(lo_half, outs[0], outs[1]).astype(BF16)


def _swa(z, sinks, B, S, col0):
    T = B * S
    nq = S // BLOCK
    qb = col0 // SWA_Q_W
    kb = (col0 + SWA_Q_W) // SWA_KV_EXP_W
    vb = kb + 1
    slopes = tuple(2.0 ** (-8.0 * (h + 1) / SWA_Q_HEADS) for h in range(SWA_Q_HEADS))
    cur = lambda b, i: b * nq + i
    prev = lambda b, i: jnp.maximum(b * nq + i - 1, 0)
    return pl.pallas_call(
        functools.partial(_swa_kernel, slopes=slopes),
        grid=(B, nq),
        in_specs=[
            pl.BlockSpec(memory_space=pltpu.SMEM),
            pl.BlockSpec((BLOCK, SWA_Q_W), lambda b, i: (cur(b, i), qb)),
            pl.BlockSpec((BLOCK, SWA_KV_EXP_W), lambda b, i: (prev(b, i), kb)),
            pl.BlockSpec((BLOCK, SWA_KV_EXP_W), lambda b, i: (cur(b, i), kb)),
            pl.BlockSpec((BLOCK, SWA_KV_EXP_W), lambda b, i: (prev(b, i), vb)),
            pl.BlockSpec((BLOCK, SWA_KV_EXP_W), lambda b, i: (cur(b, i), vb)),
        ],
        out_specs=pl.BlockSpec((BLOCK, SWA_Q_W), lambda b, i: (cur(b, i), 0)),
        out_shape=jax.ShapeDtypeStruct((T, SWA_Q_W), BF16),
        compiler_params=_cparams(2),
        name="swa",
    )(sinks, z, z, z, z, z)


def _split3(x):
    hi = x.astype(BF16).astype(F32)
    r1 = x - hi
    mid = r1.astype(BF16).astype(F32)
    lo = (r1 - mid).astype(BF16).astype(F32)
    return hi, mid, lo


def _split3_adjacent(c, lane):
    hi, mid, lo = _split3(jnp.where((lane & (FORGET_STRIDE - 1)) == 0, c, 0.0))
    return hi + pltpu.roll(mid, 1, axis=1) + pltpu.roll(lo, 2, axis=1)


def _aug_lanes(lane, terms, h, val0, one0):
    ones = jnp.where((lane >= one0) & (lane < one0 + 3), 1.0, 0.0)
    moved = pltpu.roll(terms, (val0 - h * FORGET_STRIDE) % LANES, axis=1)
    return jnp.where((lane >= val0) & (lane < val0 + 3), moved, ones)


def _fox_kernel(q_ref, k_ref, v_ref, c_ref, y_ref, ka_ref, va_ref):
    qi = pl.program_id(1)
    S = k_ref.shape[0]
    lane_k = lax.broadcasted_iota(I32, (FOX_TK, LANES), 1)
    lane_q = lax.broadcasted_iota(I32, (BLOCK, LANES), 1)

    def head_lanes(h):
        return slice(h * LANES, (h + 1) * LANES)

    def spare0(h):
        return HEAD_DIM if h % 2 == 0 else 0

    def keep(lane, h):
        return (lane < HEAD_DIM) if h % 2 == 0 else (lane >= HEAD_DIM)

    @pl.when(qi == 0)
    def _():
        def build(r, carry):
            r0 = pl.multiple_of(r * FOX_TK, FOX_TK)
            parts = _split3_adjacent(-c_ref[pl.ds(r0, FOX_TK), :], lane_k)
            for pair in range(FOX_HEADS // 2):
                kp = k_ref[pl.ds(r0, FOX_TK), pair * LANES:(pair + 1) * LANES].astype(F32)
                vp = v_ref[pl.ds(r0, FOX_TK), pair * LANES:(pair + 1) * LANES]
                for h in (2 * pair, 2 * pair + 1):
                    extra = _aug_lanes(lane_k, parts, h, spare0(h), spare0(h) + 3)
                    ka_ref[pl.ds(r0, FOX_TK), head_lanes(h)] = jnp.where(keep(lane_k, h), kp, extra).astype(BF16)
                    va_ref[pl.ds(r0, FOX_TK), head_lanes(h)] = jnp.where(keep(lane_k, h), vp, jnp.ones_like(vp))
            return carry

        lax.fori_loop(0, S // FOX_TK, build, 0)

    n_full = qi // (FOX_TK // BLOCK)
    row = qi * BLOCK + lax.broadcasted_iota(I32, (BLOCK, FOX_TK), 0)
    col = lax.broadcasted_iota(I32, (BLOCK, FOX_TK), 1)
    ct_parts = _split3_adjacent(c_ref[pl.ds(pl.multiple_of(qi * BLOCK, BLOCK), BLOCK), :], lane_q)

    hpl = 2 * FOX_PAIRS_PER_LOOP
    for h0 in range(0, FOX_HEADS, hpl):
        heads = list(range(h0, h0 + hpl))
        qas = []
        for h in heads:
            q2 = q_ref[:, (h // 2) * LANES:(h // 2 + 1) * LANES].astype(F32) * (HEAD_DIM ** -0.5)
            extra = _aug_lanes(lane_q, ct_parts, h, spare0(h) + 3, spare0(h))
            qas.append(jnp.where(keep(lane_q, h), q2, extra).astype(BF16))

        def step(j, carry, masked, heads=heads, qas=qas):
            k0 = pl.multiple_of(j * FOX_TK, FOX_TK)
            out = []
            for i, h in enumerate(heads):
                m, acc = carry[2 * i:2 * i + 2]
                s = lax.dot_general(qas[i], ka_ref[pl.ds(k0, FOX_TK), head_lanes(h)],
                                    (((1,), (1,)), ((), ())), preferred_element_type=F32)
                if masked:
                    s = jnp.where(col + j * FOX_TK <= row, s, NEG_INF)
                m_new = jnp.maximum(m, jnp.max(s, axis=-1, keepdims=True))
                p = jnp.exp(s - m_new).astype(BF16)
                acc = jnp.exp(m - m_new) * acc + jnp.dot(p, va_ref[pl.ds(k0, FOX_TK), head_lanes(h)],
                                                         preferred_element_type=F32)
                out += [m_new, acc]
            return tuple(out)

        init = (jnp.full((BLOCK, 1), -1e30, F32), jnp.zeros((BLOCK, LANES), F32)) * hpl
        carry = lax.fori_loop(0, n_full, functools.partial(step, masked=False), init)
        carry = step(n_full, carry, True)
        for i in range(0, hpl, 2):
            acc_lo, acc_hi = carry[2 * i + 1], carry[2 * i + 3]
            o_lo = acc_lo / acc_lo[:, LANES - 1:LANES]
            o_hi = acc_hi / acc_hi[:, 0:1]
            pair = heads[i] // 2
            y_ref[:, pair * LANES:(pair + 1) * LANES] = jnp.where(lane_q < HEAD_DIM, o_lo, o_hi).astype(BF16)


def _fox(z, ccol, B, S, col0):
    T = B * S
    nq = S // BLOCK
    qb = col0 // FOX_W
    return pl.pallas_call(
        _fox_kernel,
        grid=(B, nq),
        in_specs=[
            pl.BlockSpec((BLOCK, FOX_W), lambda b, i: (b * nq + i, qb)),
            pl.BlockSpec((S, FOX_W), lambda b, i: (b, qb + 1)),
            pl.BlockSpec((S, FOX_W), lambda b, i: (b, qb + 2)),
            pl.BlockSpec((S, LANES), lambda b, i: (b, 0)),
        ],
        out_specs=pl.BlockSpec((BLOCK, FOX_W), lambda b, i: (b * nq + i, 0)),
        out_shape=jax.ShapeDtypeStruct((T, FOX_W), BF16),
        scratch_shapes=[pltpu.VMEM((S, FOX_HEADS * LANES), BF16), pltpu.VMEM((S, FOX_HEADS * LANES), BF16)],
        compiler_params=_cparams(2),
        name="fox",
    )(z, z, z, ccol)


def _first_argmax(vals, iota, n):
    m = jnp.max(vals, axis=0, keepdims=True)
    idx = jnp.min(jnp.where(vals == m, iota, n), axis=0, keepdims=True)
    return m, idx


def _route(logits_t, rb):
    E, tm = logits_t.shape
    gs = E // N_GROUPS
    scores = jax.nn.sigmoid(logits_t)
    biased = scores + rb
    iota_g = lax.broadcasted_iota(I32, (gs, tm), 0)
    iota_n = lax.broadcasted_iota(I32, (N_GROUPS, tm), 0)
    grp = jnp.zeros((N_GROUPS, tm), F32)
    for g in range(N_GROUPS):
        blk = biased[g * gs:(g + 1) * gs, :]
        m1, i1 = _first_argmax(blk, iota_g, gs)
        m2 = jnp.max(jnp.where(iota_g == i1, NEG_INF, blk), axis=0, keepdims=True)
        grp = jnp.where(iota_n == g, m1 + m2, grp)
    gsel = jnp.zeros((N_GROUPS, tm), jnp.bool_)
    cur = grp
    for _ in range(TOPK_GROUPS):
        _, gi = _first_argmax(cur, iota_n, N_GROUPS)
        oh = iota_n == gi
        gsel = gsel | oh
        cur = jnp.where(oh, NEG_INF, cur)
    gself = gsel.astype(F32)
    emask = jnp.concatenate(
        [jnp.broadcast_to(gself[g:g + 1, :], (gs, tm)) for g in range(N_GROUPS)], axis=0) > 0.5
    cur = jnp.where(emask, biased, NEG_INF)
    iota_e = lax.broadcasted_iota(I32, (E, tm), 0)
    sel = jnp.zeros((E, tm), jnp.bool_)
    picks = []
    for _ in range(TOP_K):
        _, ei = _first_argmax(cur, iota_e, E)
        oh = iota_e == ei
        sel = sel | oh
        cur = jnp.where(oh, NEG_INF, cur)
        picks.append(ei)
    w = jnp.where(sel, scores, 0.0)
    w = w / jnp.sum(w, axis=0, keepdims=True) * ROUTED_SCALE
    return sel, w, picks, iota_e


def _post_kernel(ysw_ref, yfx_ref, gsw_ref, gfx_ref, x_ref, wsw_ref, wfx_ref, wo_ref, g2_ref,
                 wrt_ref, rb_ref, h1_ref, vp_ref, eidx_ref, rank_ref, wgt_ref, cnt_ref, carry_ref, slab_scr):
    @pl.when(pl.program_id(0) == 0)
    def _():
        carry_ref[...] = jnp.zeros_like(carry_ref)

    tm = x_ref.shape[0]
    a = jnp.dot(ysw_ref[...], wsw_ref[...], preferred_element_type=F32)
    b = jnp.dot(yfx_ref[...], wfx_ref[...], preferred_element_type=F32)
    mix = gsw_ref[...].astype(F32) * a + gfx_ref[...].astype(F32) * b
    h1 = x_ref[...] + jnp.dot(mix.astype(BF16), wo_ref[...], preferred_element_type=F32)
    h1_ref[...] = h1
    v = _rms(h1, g2_ref[...])
    _store_slabs(vp_ref, slab_scr, v)

    logits_t = lax.dot_general(wrt_ref[...], v, (((1,), (1,)), ((), ())),
                               precision=HI, preferred_element_type=F32)
    sel, w, picks, iota_e = _route(logits_t, rb_ref[...])

    before = (lax.broadcasted_iota(I32, (tm, tm), 0) < lax.broadcasted_iota(I32, (tm, tm), 1)).astype(BF16)
    rank = jnp.dot(sel.astype(BF16), before, preferred_element_type=F32) + carry_ref[...]
    carry_ref[...] = carry_ref[...] + jnp.sum(sel.astype(F32), axis=1, keepdims=True)
    cnt_ref[...] = carry_ref[...]

    iota_k = lax.broadcasted_iota(I32, (TOP_K, tm), 0)
    eidx = jnp.zeros((TOP_K, tm), I32)
    rnk = jnp.zeros((TOP_K, tm), F32)
    wgt = jnp.zeros((TOP_K, tm), F32)
    for k, ei in enumerate(picks):
        oh = iota_e == ei
        eidx = jnp.where(iota_k == k, ei, eidx)
        rnk = jnp.where(iota_k == k, jnp.sum(jnp.where(oh, rank, 0.0), axis=0, keepdims=True), rnk)
        wgt = jnp.where(iota_k == k, jnp.sum(jnp.where(oh, w, 0.0), axis=0, keepdims=True), wgt)
    eidx_ref[...] = eidx
    rank_ref[...] = rnk.astype(I32)
    wgt_ref[...] = wgt


def _post(ysw, yfx, z, x2, wsw, wfx, wo, g2, wrt, rb, tm):
    T, D = x2.shape
    E = wrt.shape[0]
    nch = D // LANES
    const = lambda i: (0, 0)
    return pl.pallas_call(
        _post_kernel,
        grid=(T // tm,),
        in_specs=[
            pl.BlockSpec((tm, SWA_Q_W), lambda i: (i, 0)),
            pl.BlockSpec((tm, FOX_W), lambda i: (i, 0)),
            pl.BlockSpec((tm, D), lambda i: (i, 0)),
            pl.BlockSpec((tm, D), lambda i: (i, 1)),
            pl.BlockSpec((tm, D), lambda i: (i, 0)),
            pl.BlockSpec((SWA_Q_W, D), const),
            pl.BlockSpec((FOX_W, D), const),
            pl.BlockSpec((D, D), const),
            pl.BlockSpec((1, D), const),
            pl.BlockSpec((E, D), const),
            pl.BlockSpec((E, 1), const),
        ],
        out_specs=[
            pl.BlockSpec((tm, D), lambda i: (i, 0)),
            pl.BlockSpec((tm * nch, LANES), lambda i: (i, 0)),
            pl.BlockSpec((TOP_K, tm), lambda i: (0, i)),
            pl.BlockSpec((TOP_K, tm), lambda i: (0, i)),
            pl.BlockSpec((TOP_K, tm), lambda i: (0, i)),
            pl.BlockSpec((E, 1), const),
        ],
        out_shape=[
            jax.ShapeDtypeStruct((T, D), F32),
            jax.ShapeDtypeStruct((T * nch, LANES), BF16),
            jax.ShapeDtypeStruct((TOP_K, T), I32),
            jax.ShapeDtypeStruct((TOP_K, T), I32),
            jax.ShapeDtypeStruct((TOP_K, T), F32),
            jax.ShapeDtypeStruct((E, 1), F32),
        ],
        scratch_shapes=[pltpu.VMEM((E, 1), F32), pltpu.VMEM((2, tm * nch // 2, LANES), F32)],
        compiler_params=_cparams(1),
        name="post_mixer",
    )(ysw, yfx, z, z, x2, wsw, wfx, wo, g2, wrt, rb)


def _dispatch_kernel(dest_ref, vp_ref, xs_in, xs_out, sem, *, td, nch):
    del xs_in

    def row_copy(src_row, dst_row):
        return pltpu.make_async_copy(vp_ref.at[pl.ds(src_row, nch), :],
                                     xs_out.at[pl.ds(dst_row, nch), :], sem)

    def issue(t, c):
        src = pl.multiple_of(t * nch, nch)
        for k in range(TOP_K):
            row_copy(src, pl.multiple_of(dest_ref[t * TOP_K + k] * nch, nch)).start(priority=k % 2)
        return c

    lax.fori_loop(0, td, issue, 0)

    def drain(t, c):
        for k in range(TOP_K):
            row_copy(0, 0).wait()
        return c

    lax.fori_loop(0, td, drain, 0)


def _dispatch(dest_flat, vp, n_rows, td):
    nch = vp.shape[0] * TOP_K // dest_flat.shape[0]
    T = vp.shape[0] // nch
    xs0 = jnp.zeros((n_rows * nch, LANES), BF16)
    return pl.pallas_call(
        functools.partial(_dispatch_kernel, td=td, nch=nch),
        grid=(T // td,),
        in_specs=[
            pl.BlockSpec((td * TOP_K,), lambda i: (i,), memory_space=pltpu.SMEM),
            pl.BlockSpec((td * nch, LANES), lambda i: (i, 0)),
            pl.BlockSpec(memory_space=pl.ANY),
        ],
        out_specs=pl.BlockSpec(memory_space=pl.ANY),
        out_shape=jax.ShapeDtypeStruct((n_rows * nch, LANES), BF16),
        scratch_shapes=[pltpu.SemaphoreType.DMA(())],
        input_output_aliases={2: 0},
        compiler_params=_cparams(1),
        name="dispatch",
    )(dest_flat, vp, xs0)


def _experts_kernel(be_ref, nu_ref, x_ref, wg_ref, wu_ref, wd_ref, y_ref, wg_s, wu_s, wd_s, slab_scr,
                    *, rows, nch):
    b = pl.program_id(0)

    @pl.when(b < nu_ref[0])
    def _():
        @pl.when((b == 0) | (be_ref[b] != be_ref[jnp.maximum(b - 1, 0)]))
        def _():
            wg_s[...] = wg_ref[0].astype(BF16)
            wu_s[...] = wu_ref[0].astype(BF16)
            wd_s[...] = wd_ref[0].astype(BF16)

        x = jnp.concatenate([c.astype(BF16) for c in _load_slabs(x_ref, 0, slab_scr, rows, nch)], axis=1)
        hg = jnp.dot(x, wg_s[...], preferred_element_type=F32)
        hu = jnp.dot(x, wu_s[...], preferred_element_type=F32)
        act = (jax.nn.silu(hg) * hu).astype(BF16)
        y = jnp.dot(act, wd_s[...], preferred_element_type=F32)
        _store_slabs(y_ref, slab_scr, y)

    @pl.when(b >= nu_ref[0])
    def _():
        y_ref[...] = jnp.zeros_like(y_ref)


def _experts(blk_exp, n_used, xs, wg, wu, wd, rows):
    E, D, FF = wg.shape
    nch = D // LANES
    nblk = xs.shape[0] // (rows * nch)
    xmap = lambda b, be, nu: (jnp.minimum(b, nu[0] - 1), 0)
    wmap = lambda b, be, nu: (be[b], 0, 0)
    return pl.pallas_call(
        functools.partial(_experts_kernel, rows=rows, nch=nch),
        grid_spec=pltpu.PrefetchScalarGridSpec(
            num_scalar_prefetch=2,
            grid=(nblk,),
            in_specs=[
                pl.BlockSpec((rows * nch, LANES), xmap),
                pl.BlockSpec((1, D, FF), wmap),
                pl.BlockSpec((1, D, FF), wmap),
                pl.BlockSpec((1, FF, D), wmap),
            ],
            out_specs=pl.BlockSpec((rows * nch, LANES), lambda b, be, nu: (b, 0)),
            scratch_shapes=[pltpu.VMEM((D, FF), BF16), pltpu.VMEM((D, FF), BF16), pltpu.VMEM((FF, D), BF16),
                            pltpu.VMEM((2, rows * nch // 2, LANES), F32)],
        ),
        out_shape=jax.ShapeDtypeStruct(xs.shape, BF16),
        compiler_params=_cparams(1),
        name="experts",
    )(blk_exp, n_used, xs, wg, wu, wd)


def _combine_kernel(dest_ref, dest_next_ref, wtok_ref, h1_ref, vp_ref, wsg_ref, wsu_ref, wsd_ref, gf_ref,
                    ys_hbm, out_ref, buf, slab_scr, sems, *, tc, nch):
    i = pl.program_id(0)
    slot = i % 2

    def row_copy(src_row, s, dst_row):
        return pltpu.make_async_copy(ys_hbm.at[pl.ds(src_row, nch), :],
                                     buf.at[s].at[pl.ds(dst_row, nch), :], sems.at[s])

    def gather(idx_ref, s):
        def issue(t, c):
            for k in range(TOP_K):
                row_copy(pl.multiple_of(idx_ref[t * TOP_K + k] * nch, nch), s,
                         pl.multiple_of((k * tc + t) * nch, nch)).start(priority=k % 2)
            return c

        lax.fori_loop(0, tc, issue, 0)

    @pl.when(i == 0)
    def _():
        gather(dest_ref, 0)

    @pl.when(i + 1 < pl.num_programs(0))
    def _():
        gather(dest_next_ref, 1 - slot)

    x = jnp.concatenate([c.astype(BF16) for c in _load_slabs(vp_ref, 0, slab_scr, tc, nch)], axis=1)
    hg = jnp.dot(x, wsg_ref[...], preferred_element_type=F32)
    hu = jnp.dot(x, wsu_ref[...], preferred_element_type=F32)
    act = (jax.nn.silu(hg) * hu).astype(BF16)
    out_ref[...] = h1_ref[...] + jnp.dot(act, wsd_ref[...], preferred_element_type=F32)

    def drain(t, c):
        for k in range(TOP_K):
            row_copy(0, slot, 0).wait()
        return c

    lax.fori_loop(0, tc, drain, 0)

    rows = buf.at[slot]
    for k in range(TOP_K):
        wk = jnp.broadcast_to(wtok_ref[:, k:k + 1], (tc, LANES))
        _fill_slab_scratch(rows, k * tc * nch, slab_scr, tc, nch)
        for j in range(nch):
            out_ref[:, j * LANES:(j + 1) * LANES] += wk * _slab_chunk(slab_scr, j, tc, nch)
    out_ref[...] = _rms(out_ref[...], gf_ref[...])


def _combine(dest_flat, wtok, h1, vp, wsg, wsu, wsd, gf, ys, tc):
    T, D = h1.shape
    FF = wsg.shape[1]
    nch = D // LANES
    const = lambda i: (0, 0)
    n_steps = T // tc
    return pl.pallas_call(
        functools.partial(_combine_kernel, tc=tc, nch=nch),
        grid=(n_steps,),
        in_specs=[
            pl.BlockSpec((tc * TOP_K,), lambda i: (i,), memory_space=pltpu.SMEM),
            pl.BlockSpec((tc * TOP_K,), lambda i: (jnp.minimum(i + 1, n_steps - 1),), memory_space=pltpu.SMEM),
            pl.BlockSpec((tc, TOP_K), lambda i: (i, 0)),
            pl.BlockSpec((tc, D), lambda i: (i, 0)),
            pl.BlockSpec((tc * nch, LANES), lambda i: (i, 0)),
            pl.BlockSpec((D, FF), const),
            pl.BlockSpec((D, FF), const),
            pl.BlockSpec((FF, D), const),
            pl.BlockSpec((1, D), const),
            pl.BlockSpec(memory_space=pl.ANY),
        ],
        out_specs=pl.BlockSpec((tc, D), lambda i: (i, 0)),
        out_shape=jax.ShapeDtypeStruct((T, D), F32),
        scratch_shapes=[pltpu.VMEM((2, TOP_K * tc * nch, LANES), BF16),
                        pltpu.VMEM((2, tc * nch // 2, LANES), F32), pltpu.SemaphoreType.DMA((2,))],
        compiler_params=_cparams(1),
        name="combine",
    )(dest_flat, dest_flat, wtok, h1, vp, wsg, wsu, wsd, gf, ys)


def _layer(h, attn_norm_g, w_in, forget_bias, sinks, w_branch_swa, w_branch_fox, w_out, ffn_norm_g,
           w_router, router_bias, w_exp_gate, w_exp_up, w_exp_down, w_sh_gate, w_sh_up, w_sh_down, out_g):
    B, S, D = h.shape
    T = B * S
    E = w_router.shape[1]
    x2 = h.reshape(T, D)

    o = np.cumsum([0, SWA_Q_W, SWA_KV_W, SWA_KV_W, FOX_W, FOX_W, FOX_W, FORGET_W, D, D])
    aq, ak, av, fq, fk, fv, fg, gsw, gfx = [w_in[:, o[i]:o[i + 1]] for i in range(9)]

    def dup(w):
        w4 = w.reshape(D, SWA_KV_HEADS, 1, HEAD_DIM)
        return jnp.broadcast_to(w4, (D, SWA_KV_HEADS, 2, HEAD_DIM)).reshape(D, SWA_KV_EXP_W)

    w_main = jnp.concatenate([gsw, gfx, aq, dup(ak), dup(av), fq, fk, fv], axis=1).astype(BF16)
    spread = lambda a: jnp.pad(a[..., None], [(0, 0)] * a.ndim + [(0, FORGET_STRIDE - 1)]).reshape(*a.shape[:-1], LANES)
    w_f = spread(fg).astype(BF16)
    fb = spread(forget_bias.astype(F32).reshape(1, FORGET_W))

    z, f = _inproj(x2, attn_norm_g.reshape(1, D), w_main, w_f, tm=min(1024, T))
    ccol = _forget_cumsum(f, fb, B, S)
    y_swa = _swa(z, sinks.astype(F32), B, S, col0=2 * D)
    y_fox = _fox(z, ccol, B, S, col0=2 * D + SWA_Q_W + 2 * SWA_KV_EXP_W)

    h1, vp, eidx, rank, wgt, cnt = _post(
        y_swa, y_fox, z, x2, w_branch_swa.astype(BF16), w_branch_fox.astype(BF16), w_out.astype(BF16),
        ffn_norm_g.reshape(1, D), w_router.T, router_bias.reshape(E, 1).astype(F32), tm=min(256, T))

    rows = MOE_ROWS
    n_blocks = T * TOP_K // rows + E
    counts = cnt[:, 0].astype(I32)
    padded = (counts + rows - 1) // rows * rows
    pad_end = jnp.cumsum(padded)
    pad_start = pad_end - padded
    experts = jnp.arange(E, dtype=I32)[:, None, None]
    start_of = jnp.sum(jnp.where(eidx[None] == experts, pad_start[:, None, None], 0), axis=0)
    dest_flat = (start_of + rank).T.reshape(T * TOP_K)
    n_used = (pad_end[-1] // rows).astype(I32)
    blk = jnp.arange(n_blocks, dtype=I32)
    blk_exp = jnp.minimum(jnp.sum((pad_end[None, :] <= blk[:, None] * rows).astype(I32), axis=1), E - 1)
    blk_exp = jnp.where(blk < n_used, blk_exp, blk_exp[n_used - 1])

    xs = _dispatch(dest_flat, vp, n_blocks * rows, td=min(256, T))
    ys = _experts(blk_exp, n_used.reshape(1), xs, w_exp_gate, w_exp_up, w_exp_down, rows)
    out = _combine(dest_flat, wgt.T, h1, vp, w_sh_gate.astype(BF16), w_sh_up.astype(BF16),
                   w_sh_down.astype(BF16), out_g.reshape(1, D), ys, tc=min(256, T))
    return out.reshape(B, S, D)


def kernel(x, attn_norm_g, w_in, forget_bias, sinks, w_branch_swa, w_branch_fox, w_out, ffn_norm_g,
           w_router, router_bias, w_exp_gate, w_exp_up, w_exp_down, w_sh_gate, w_sh_up, w_sh_down,
           final_norm_g):
    depth = w_in.shape[0]
    assert depth == 1, "the final norm is fused into the last layer's combine"
    return _layer(x, attn_norm_g[0], w_in[0], forget_bias[0], sinks[0], w_branch_swa[0], w_branch_fox[0],
                  w_out[0], ffn_norm_g[0], w_router[0], router_bias[0], w_exp_gate[0], w_exp_up[0],
                  w_exp_down[0], w_sh_gate[0], w_sh_up[0], w_sh_down[0], final_norm_g)
```

```python
import functools

import numpy as np
import jax
import jax.numpy as jnp
from jax import lax
from jax.experimental import pallas as pl
from jax.experimental.pallas import tpu as pltpu

F32 = jnp.float32
BF16 = jnp.bfloat16
I32 = jnp.int32

HEAD_DIM = 64
SWA_Q_HEADS = 16
SWA_KV_HEADS = 4
FOX_HEADS = 16
WINDOW = 128
BLOCK = 128
N_GROUPS = 8
TOPK_GROUPS = 4
TOP_K = 8
ROUTED_SCALE = 2.5
RMS_EPS = 1e-6
LANES = 128
SWA_Q_W = SWA_Q_HEADS * HEAD_DIM
SWA_KV_W = SWA_KV_HEADS * HEAD_DIM
FOX_W = FOX_HEADS * HEAD_DIM
FORGET_W = FOX_HEADS
FORGET_STRIDE = 8
SWA_KV_EXP_W = 2 * SWA_KV_W
Z_TILE = 512
FOX_TK = 256
FOX_PAIRS_PER_LOOP = 8
MOE_ROWS = 256
VMEM_LIMIT = 56 * 1024 * 1024
HI = lax.Precision.HIGHEST
NEG_INF = float("-inf")


def _cparams(n_axes):
    return pltpu.CompilerParams(dimension_semantics=("arbitrary",) * n_axes,
                                vmem_limit_bytes=VMEM_LIMIT)


def _rms(x, g):
    return x * lax.rsqrt(jnp.mean(x * x, axis=-1, keepdims=True) + RMS_EPS) * g


def _store_slabs(ref, scr, x):
    rows = x.shape[0]
    n = x.shape[1] // LANES
    hn = n // 2
    for j in range(n):
        scr[j // hn, pl.ds(j % hn, rows, stride=hn), :] = x[:, j * LANES:(j + 1) * LANES]
    halves = [scr[i].reshape(rows, 1, hn, LANES) for i in range(2)]
    ref[...] = jnp.concatenate(halves, axis=1).reshape(rows * n, LANES).astype(BF16)


def _fill_slab_scratch(ref, base, scr, rows, n):
    hn = n // 2
    xf = ref[pl.ds(base, rows * n), :].astype(F32).reshape(rows, 2, hn, LANES)
    for i in range(2):
        scr[i] = xf[:, i].reshape(rows * hn, LANES)


def _slab_chunk(scr, j, rows, n):
    hn = n // 2
    return scr[j // hn, pl.ds(j % hn, rows, stride=hn), :]


def _load_slabs(ref, base, scr, rows, n):
    _fill_slab_scratch(ref, base, scr, rows, n)
    return [_slab_chunk(scr, j, rows, n) for j in range(n)]


def _inproj_kernel(x_ref, g_ref, w_ref, wf_ref, z_ref, f_ref, u_scr, *, n_gate_tiles):
    j = pl.program_id(1)

    @pl.when(j == 0)
    def _():
        u = _rms(x_ref[...], g_ref[...]).astype(BF16)
        u_scr[...] = u
        f_ref[...] = jnp.dot(u, wf_ref[...], preferred_element_type=F32)

    z = jnp.dot(u_scr[...], w_ref[...], preferred_element_type=F32)

    @pl.when(j < n_gate_tiles)
    def _():
        z_ref[...] = jax.nn.sigmoid(z).astype(BF16)

    @pl.when(j >= n_gate_tiles)
    def _():
        z_ref[...] = z.astype(BF16)


def _inproj(x2, g, w_main, w_f, tm):
    T, D = x2.shape
    ZW = w_main.shape[1]
    return pl.pallas_call(
        functools.partial(_inproj_kernel, n_gate_tiles=2 * D // Z_TILE),
        grid=(T // tm, ZW // Z_TILE),
        in_specs=[
            pl.BlockSpec((tm, D), lambda i, j: (i, 0)),
            pl.BlockSpec((1, D), lambda i, j: (0, 0)),
            pl.BlockSpec((D, Z_TILE), lambda i, j: (0, j)),
            pl.BlockSpec((D, LANES), lambda i, j: (0, 0)),
        ],
        out_specs=[
            pl.BlockSpec((tm, Z_TILE), lambda i, j: (i, j)),
            pl.BlockSpec((tm, LANES), lambda i, j: (i, 0)),
        ],
        out_shape=[jax.ShapeDtypeStruct((T, ZW), BF16), jax.ShapeDtypeStruct((T, LANES), F32)],
        scratch_shapes=[pltpu.VMEM((tm, D), BF16)],
        compiler_params=_cparams(2),
        name="inproj",
    )(x2, g, w_main, w_f)


def _fcum_kernel(f_ref, b_ref, ccol_ref, carry_ref):
    @pl.when(pl.program_id(1) == 0)
    def _():
        carry_ref[...] = jnp.zeros_like(carry_ref)

    a = f_ref[...] + b_ref[...]
    ls = jnp.minimum(a, 0.0) - jnp.log1p(jnp.exp(-jnp.abs(a)))
    n = ls.shape[0]
    tri = (lax.broadcasted_iota(I32, (n, n), 1) <= lax.broadcasted_iota(I32, (n, n), 0)).astype(F32)
    c = jnp.dot(tri, ls, precision=HI, preferred_element_type=F32) + carry_ref[...]
    ccol_ref[...] = c
    carry_ref[...] = c[n - 1:n, :]


def _forget_cumsum(f, bias, B, S):
    T = B * S
    nb = S // FOX_TK
    return pl.pallas_call(
        _fcum_kernel,
        grid=(B, nb),
        in_specs=[
            pl.BlockSpec((FOX_TK, LANES), lambda b, j: (b * nb + j, 0)),
            pl.BlockSpec((1, LANES), lambda b, j: (0, 0)),
        ],
        out_specs=pl.BlockSpec((FOX_TK, LANES), lambda b, j: (b * nb + j, 0)),
        out_shape=jax.ShapeDtypeStruct((T, LANES), F32),
        scratch_shapes=[pltpu.VMEM((1, LANES), F32)],
        compiler_params=_cparams(2),
        name="forget_cumsum",
    )(f, bias)


def _swa_kernel(sink_ref, q_ref, kp_ref, kc_ref, vp_ref, vc_ref, y_ref, *, slopes):
    qi = pl.program_id(1)
    kk = jnp.concatenate([kp_ref[...], kc_ref[...]], axis=0)
    vv = jnp.concatenate([vp_ref[...], vc_ref[...]], axis=0)
    row = lax.broadcasted_iota(I32, (BLOCK, 2 * BLOCK), 0)
    col = lax.broadcasted_iota(I32, (BLOCK, 2 * BLOCK), 1)
    dist = row + BLOCK - col
    valid = (dist >= 0) & (dist < WINDOW) & ((qi > 0) | (col >= BLOCK))
    distf = dist.astype(F32)
    lo_half = lax.broadcasted_iota(I32, (BLOCK, LANES), 1) < HEAD_DIM
    group = SWA_Q_HEADS // SWA_KV_HEADS
    for pair in range(SWA_Q_HEADS // 2):
        q2 = q_ref[:, pair * LANES:(pair + 1) * LANES]
        g = (2 * pair) // group
        kg = kk[:, g * LANES:(g + 1) * LANES]
        vg = vv[:, g * LANES:(g + 1) * LANES]
        outs = []
        for half in range(2):
            h = 2 * pair + half
            keep = lo_half if half == 0 else jnp.logical_not(lo_half)
            qm = jnp.where(keep, q2, jnp.zeros_like(q2)) * jnp.asarray(HEAD_DIM ** -0.5, BF16)
            s = lax.dot_general(qm, kg, (((1,), (1,)), ((), ())), preferred_element_type=F32)
            s = jnp.where(valid, s - slopes[h] * distf, NEG_INF)
            sink = sink_ref[h]
            m = jnp.maximum(jnp.max(s, axis=-1, keepdims=True), sink)
            p = jnp.exp(s - m)
            denom = jnp.sum(p, axis=-1, keepdims=True) + jnp.exp(sink - m)
            o = jnp.dot(p.astype(BF16), vg, preferred_element_type=F32)
            outs.append(o / denom)
        y_ref[:, pair * LANES:(pair + 1) * LANES] = jnp.where(lo_half, outs[0], outs[1]).astype(BF16)


def _swa(z, sinks, B, S, col0):
    T = B * S
    nq = S // BLOCK
    qb = col0 // SWA_Q_W
    kb = (col0 + SWA_Q_W) // SWA_KV_EXP_W
    vb = kb + 1
    slopes = tuple(2.0 ** (-8.0 * (h + 1) / SWA_Q_HEADS) for h in range(SWA_Q_HEADS))
    cur = lambda b, i: b * nq + i
    prev = lambda b, i: jnp.maximum(b * nq + i - 1, 0)
    return pl.pallas_call(
        functools.partial(_swa_kernel, slopes=slopes),
        grid=(B, nq),
        in_specs=[
            pl.BlockSpec(memory_space=pltpu.SMEM),
            pl.BlockSpec((BLOCK, SWA_Q_W), lambda b, i: (cur(b, i), qb)),
            pl.BlockSpec((BLOCK, SWA_KV_EXP_W), lambda b, i: (prev(b, i), kb)),
            pl.BlockSpec((BLOCK, SWA_KV_EXP_W), lambda b, i: (cur(b, i), kb)),
            pl.BlockSpec((BLOCK, SWA_KV_EXP_W), lambda b, i: (prev(b, i), vb)),
            pl.BlockSpec((BLOCK, SWA_KV_EXP_W), lambda b, i: (cur(b, i), vb)),
        ],
        out_specs=pl.BlockSpec((BLOCK, SWA_Q_W), lambda b, i: (cur(b, i), 0)),
        out_shape=jax.ShapeDtypeStruct((T, SWA_Q_W), BF16),
        compiler_params=_cparams(2),
        name="swa",
    )(sinks, z, z, z, z, z)


def _split3(x):
    hi = x.astype(BF16).astype(F32)
    r1 = x - hi
    mid = r1.astype(BF16).astype(F32)
    lo = (r1 - mid).astype(BF16).astype(F32)
    return hi, mid, lo


def _split3_adjacent(c, lane):
    hi, mid, lo = _split3(jnp.where((lane & (FORGET_STRIDE - 1)) == 0, c, 0.0))
    return hi + pltpu.roll(mid, 1, axis=1) + pltpu.roll(lo, 2, axis=1)


def _aug_lanes(lane, terms, h, val0, one0):
    ones = jnp.where((lane >= one0) & (lane < one0 + 3), 1.0, 0.0)
    moved = pltpu.roll(terms, (val0 - h * FORGET_STRIDE) % LANES, axis=1)
    return jnp.where((lane >= val0) & (lane < val0 + 3), moved, ones)


def _fox_kernel(q_ref, k_ref, v_ref, c_ref, y_ref, ka_ref, va_ref):
    qi = pl.program_id(1)
    S = k_ref.shape[0]
    lane_k = lax.broadcasted_iota(I32, (FOX_TK, LANES), 1)
    lane_q = lax.broadcasted_iota(I32, (BLOCK, LANES), 1)

    def head_lanes(h):
        return slice(h * LANES, (h + 1) * LANES)

    def spare0(h):
        return HEAD_DIM if h % 2 == 0 else 0

    def keep(lane, h):
        return (lane < HEAD_DIM) if h % 2 == 0 else (lane >= HEAD_DIM)

    @pl.when(qi == 0)
    def _():
        def build(r, carry):
            r0 = pl.multiple_of(r * FOX_TK, FOX_TK)
            parts = _split3_adjacent(-c_ref[pl.ds(r0, FOX_TK), :], lane_k)
            for pair in range(FOX_HEADS // 2):
                kp = k_ref[pl.ds(r0, FOX_TK), pair * LANES:(pair + 1) * LANES].astype(F32)
                vp = v_ref[pl.ds(r0, FOX_TK), pair * LANES:(pair + 1) * LANES]
                for h in (2 * pair, 2 * pair + 1):
                    extra = _aug_lanes(lane_k, parts, h, spare0(h), spare0(h) + 3)
                    ka_ref[pl.ds(r0, FOX_TK), head_lanes(h)] = jnp.where(keep(lane_k, h), kp, extra).astype(BF16)
                    va_ref[pl.ds(r0, FOX_TK), head_lanes(h)] = jnp.where(keep(lane_k, h), vp, jnp.ones_like(vp))
            return carry

        lax.fori_loop(0, S // FOX_TK, build, 0)

    n_full = qi // (FOX_TK // BLOCK)
    row = qi * BLOCK + lax.broadcasted_iota(I32, (BLOCK, FOX_TK), 0)
    col = lax.broadcasted_iota(I32, (BLOCK, FOX_TK), 1)
    ct_parts = _split3_adjacent(c_ref[pl.ds(pl.multiple_of(qi * BLOCK, BLOCK), BLOCK), :], lane_q)

    hpl = 2 * FOX_PAIRS_PER_LOOP
    for h0 in range(0, FOX_HEADS, hpl):
        heads = list(range(h0, h0 + hpl))
        qas = []
        for h in heads:
            q2 = q_ref[:, (h // 2) * LANES:(h // 2 + 1) * LANES].astype(F32) * (HEAD_DIM ** -0.5)
            extra = _aug_lanes(lane_q, ct_parts, h, spare0(h) + 3, spare0(h))
            qas.append(jnp.where(keep(lane_q, h), q2, extra).astype(BF16))

        def step(j, carry, masked, heads=heads, qas=qas):
            k0 = pl.multiple_of(j * FOX_TK, FOX_TK)
            out = []
            for i, h in enumerate(heads):
                m, acc = carry[2 * i:2 * i + 2]
                s = lax.dot_general(qas[i], ka_ref[pl.ds(k0, FOX_TK), head_lanes(h)],
                                    (((1,), (1,)), ((), ())), preferred_element_type=F32)
                if masked:
                    s = jnp.where(col + j * FOX_TK <= row, s, NEG_INF)
                m_new = jnp.maximum(m, jnp.max(s, axis=-1, keepdims=True))
                p = jnp.exp(s - m_new).astype(BF16)
                acc = jnp.exp(m - m_new) * acc + jnp.dot(p, va_ref[pl.ds(k0, FOX_TK), head_lanes(h)],
                                                         preferred_element_type=F32)
                out += [m_new, acc]
            return tuple(out)

        init = (jnp.full((BLOCK, 1), -1e30, F32), jnp.zeros((BLOCK, LANES), F32)) * hpl
        carry = lax.fori_loop(0, n_full, functools.partial(step, masked=False), init)
        carry = step(n_full, carry, True)
        for i in range(0, hpl, 2):
            acc_lo, acc_hi = carry[2 * i + 1], carry[2 * i + 3]
            o_lo = acc_lo / acc_lo[:, LANES - 1:LANES]
            o_hi = acc_hi / acc_hi[:, 0:1]
            pair = heads[i] // 2
            y_ref[:, pair * LANES:(pair + 1) * LANES] = jnp.where(lane_q < HEAD_DIM, o_lo, o_hi).astype(BF16)


def _fox(z, ccol, B, S, col0):
    T = B * S
    nq = S // BLOCK
    qb = col0 // FOX_W
    return pl.pallas_call(
        _fox_kernel,
        grid=(B, nq),
        in_specs=[
            pl.BlockSpec((BLOCK, FOX_W), lambda b, i: (b * nq + i, qb)),
            pl.BlockSpec((S, FOX_W), lambda b, i: (b, qb + 1)),
            pl.BlockSpec((S, FOX_W), lambda b, i: (b, qb + 2)),
            pl.BlockSpec((S, LANES), lambda b, i: (b, 0)),
        ],
        out_specs=pl.BlockSpec((BLOCK, FOX_W), lambda b, i: (b * nq + i, 0)),
        out_shape=jax.ShapeDtypeStruct((T, FOX_W), BF16),
        scratch_shapes=[pltpu.VMEM((S, FOX_HEADS * LANES), BF16), pltpu.VMEM((S, FOX_HEADS * LANES), BF16)],
        compiler_params=_cparams(2),
        name="fox",
    )(z, z, z, ccol)


def _first_argmax(vals, iota, n):
    m = jnp.max(vals, axis=0, keepdims=True)
    idx = jnp.min(jnp.where(vals == m, iota, n), axis=0, keepdims=True)
    return m, idx


def _route(logits_t, rb):
    E, tm = logits_t.shape
    gs = E // N_GROUPS
    scores = jax.nn.sigmoid(logits_t)
    biased = scores + rb
    iota_g = lax.broadcasted_iota(I32, (gs, tm), 0)
    iota_n = lax.broadcasted_iota(I32, (N_GROUPS, tm), 0)
    grp = jnp.zeros((N_GROUPS, tm), F32)
    for g in range(N_GROUPS):
        blk = biased[g * gs:(g + 1) * gs, :]
        m1, i1 = _first_argmax(blk, iota_g, gs)
        m2 = jnp.max(jnp.where(iota_g == i1, NEG_INF, blk), axis=0, keepdims=True)
        grp = jnp.where(iota_n == g, m1 + m2, grp)
    gsel = jnp.zeros((N_GROUPS, tm), jnp.bool_)
    cur = grp
    for _ in range(TOPK_GROUPS):
        _, gi = _first_argmax(cur, iota_n, N_GROUPS)
        oh = iota_n == gi
        gsel = gsel | oh
        cur = jnp.where(oh, NEG_INF, cur)
    gself = gsel.astype(F32)
    emask = jnp.concatenate(
        [jnp.broadcast_to(gself[g:g + 1, :], (gs, tm)) for g in range(N_GROUPS)], axis=0) > 0.5
    cur = jnp.where(emask, biased, NEG_INF)
    iota_e = lax.broadcasted_iota(I32, (E, tm), 0)
    sel = jnp.zeros((E, tm), jnp.bool_)
    picks = []
    for _ in range(TOP_K):
        _, ei = _first_argmax(cur, iota_e, E)
        oh = iota_e == ei
        sel = sel | oh
        cur = jnp.where(oh, NEG_INF, cur)
        picks.append(ei)
    w = jnp.where(sel, scores, 0.0)
    w = w / jnp.sum(w, axis=0, keepdims=True) * ROUTED_SCALE
    return sel, w, picks, iota_e


def _post_kernel(ysw_ref, yfx_ref, gsw_ref, gfx_ref, x_ref, wsw_ref, wfx_ref, wo_ref, g2_ref,
                 wrt_ref, rb_ref, h1_ref, vp_ref, eidx_ref, rank_ref, wgt_ref, cnt_ref, carry_ref, slab_scr):
    @pl.when(pl.program_id(0) == 0)
    def _():
        carry_ref[...] = jnp.zeros_like(carry_ref)

    tm = x_ref.shape[0]
    a = jnp.dot(ysw_ref[...], wsw_ref[...], preferred_element_type=F32)
    b = jnp.dot(yfx_ref[...], wfx_ref[...], preferred_element_type=F32)
    mix = gsw_ref[...].astype(F32) * a + gfx_ref[...].astype(F32) * b
    h1 = x_ref[...] + jnp.dot(mix.astype(BF16), wo_ref[...], preferred_element_type=F32)
    h1_ref[...] = h1
    v = _rms(h1, g2_ref[...])
    _store_slabs(vp_ref, slab_scr, v)

    logits_t = lax.dot_general(wrt_ref[...], v, (((1,), (1,)), ((), ())),
                               precision=HI, preferred_element_type=F32)
    sel, w, picks, iota_e = _route(logits_t, rb_ref[...])

    before = (lax.broadcasted_iota(I32, (tm, tm), 0) < lax.broadcasted_iota(I32, (tm, tm), 1)).astype(BF16)
    rank = jnp.dot(sel.astype(BF16), before, preferred_element_type=F32) + carry_ref[...]
    carry_ref[...] = carry_ref[...] + jnp.sum(sel.astype(F32), axis=1, keepdims=True)
    cnt_ref[...] = carry_ref[...]

    iota_k = lax.broadcasted_iota(I32, (TOP_K, tm), 0)
    eidx = jnp.zeros((TOP_K, tm), I32)
    rnk = jnp.zeros((TOP_K, tm), F32)
    wgt = jnp.zeros((TOP_K, tm), F32)
    for k, ei in enumerate(picks):
        oh = iota_e == ei
        eidx = jnp.where(iota_k == k, ei, eidx)
        rnk = jnp.where(iota_k == k, jnp.sum(jnp.where(oh, rank, 0.0), axis=0, keepdims=True), rnk)
        wgt = jnp.where(iota_k == k, jnp.sum(jnp.where(oh, w, 0.0), axis=0, keepdims=True), wgt)
    eidx_ref[...] = eidx
    rank_ref[...] = rnk.astype(I32)
    wgt_ref[...] = wgt


def _post(ysw, yfx, z, x2, wsw, wfx, wo, g2, wrt, rb, tm):
    T, D = x2.shape
    E = wrt.shape[0]
    nch = D // LANES
    const = lambda i: (0, 0)
    return pl.pallas_call(
        _post_kernel,
        grid=(T // tm,),
        in_specs=[
            pl.BlockSpec((tm, SWA_Q_W), lambda i: (i, 0)),
            pl.BlockSpec((tm, FOX_W), lambda i: (i, 0)),
            pl.BlockSpec((tm, D), lambda i: (i, 0)),
            pl.BlockSpec((tm, D), lambda i: (i, 1)),
            pl.BlockSpec((tm, D), lambda i: (i, 0)),
            pl.BlockSpec((SWA_Q_W, D), const),
            pl.BlockSpec((FOX_W, D), const),
            pl.BlockSpec((D, D), const),
            pl.BlockSpec((1, D), const),
            pl.BlockSpec((E, D), const),
            pl.BlockSpec((E, 1), const),
        ],
        out_specs=[
            pl.BlockSpec((tm, D), lambda i: (i, 0)),
            pl.BlockSpec((tm * nch, LANES), lambda i: (i, 0)),
            pl.BlockSpec((TOP_K, tm), lambda i: (0, i)),
            pl.BlockSpec((TOP_K, tm), lambda i: (0, i)),
            pl.BlockSpec((TOP_K, tm), lambda i: (0, i)),
            pl.BlockSpec((E, 1), const),
        ],
        out_shape=[
            jax.ShapeDtypeStruct((T, D), F32),
            jax.ShapeDtypeStruct((T * nch, LANES), BF16),
            jax.ShapeDtypeStruct((TOP_K, T), I32),
            jax.ShapeDtypeStruct((TOP_K, T), I32),
            jax.ShapeDtypeStruct((TOP_K, T), F32),
            jax.ShapeDtypeStruct((E, 1), F32),
        ],
        scratch_shapes=[pltpu.VMEM((E, 1), F32), pltpu.VMEM((2, tm * nch // 2, LANES), F32)],
        compiler_params=_cparams(1),
        name="post_mixer",
    )(ysw, yfx, z, z, x2, wsw, wfx, wo, g2, wrt, rb)


def _dispatch_kernel(zblk_ref, nz_ref, dest_ref, vp_ref, xs_out, zeros, sem, zsem, *, td, nch, blk_rows):
    @pl.when(pl.program_id(0) == 0)
    def _():
        zeros[...] = jnp.zeros_like(zeros)

        def zero_copy(blk):
            dst = pl.multiple_of(blk * blk_rows * nch, blk_rows * nch)
            return pltpu.make_async_copy(zeros, xs_out.at[pl.ds(dst, blk_rows * nch), :], zsem)

        def zstart(i, c):
            zero_copy(zblk_ref[i]).start()
            return c

        def zwait(i, c):
            zero_copy(0).wait()
            return c

        lax.fori_loop(0, nz_ref[0], zstart, 0)
        lax.fori_loop(0, nz_ref[0], zwait, 0)

    def row_copy(src_row, dst_row):
        return pltpu.make_async_copy(vp_ref.at[pl.ds(src_row, nch), :],
                                     xs_out.at[pl.ds(dst_row, nch), :], sem)

    def issue(t, c):
        src = pl.multiple_of(t * nch, nch)
        for k in range(TOP_K):
            row_copy(src, pl.multiple_of(dest_ref[t * TOP_K + k] * nch, nch)).start(priority=k % 2)
        return c

    lax.fori_loop(0, td, issue, 0)

    def drain(t, c):
        for k in range(TOP_K):
            row_copy(0, 0).wait()
        return c

    lax.fori_loop(0, td, drain, 0)


def _dispatch(zero_blocks, n_zero, dest_flat, vp, n_rows, td, blk_rows):
    nch = vp.shape[0] * TOP_K // dest_flat.shape[0]
    T = vp.shape[0] // nch
    return pl.pallas_call(
        functools.partial(_dispatch_kernel, td=td, nch=nch, blk_rows=blk_rows),
        grid_spec=pltpu.PrefetchScalarGridSpec(
            num_scalar_prefetch=2,
            grid=(T // td,),
            in_specs=[
                pl.BlockSpec((td * TOP_K,), lambda i, zb, nz: (i,), memory_space=pltpu.SMEM),
                pl.BlockSpec((td * nch, LANES), lambda i, zb, nz: (i, 0)),
            ],
            out_specs=pl.BlockSpec(memory_space=pl.ANY),
            scratch_shapes=[pltpu.VMEM((blk_rows * nch, LANES), BF16), pltpu.SemaphoreType.DMA(()),
                            pltpu.SemaphoreType.DMA(())],
        ),
        out_shape=jax.ShapeDtypeStruct((n_rows * nch, LANES), BF16),
        compiler_params=_cparams(1),
        name="dispatch",
    )(zero_blocks, n_zero, dest_flat, vp)


def _experts_kernel(bo_ref, el_ref, nu_ref, ne_ref, x_ref, wg_hbm, wu_hbm, wd_hbm, y_ref,
                    wg_f, wu_f, wd_f, wg_s, wu_s, wd_s, slab_scr, sems, *, rows, nch):
    b = pl.program_id(0)
    weights = ((wg_hbm, wg_f, wg_s), (wu_hbm, wu_f, wu_s), (wd_hbm, wd_f, wd_s))

    def fetch(o, slot):
        e = el_ref[o]
        return [pltpu.make_async_copy(w_hbm.at[e], w_f.at[slot], sems.at[slot, i])
                for i, (w_hbm, w_f, _) in enumerate(weights)]

    @pl.when(b == 0)
    def _():
        for copy in fetch(0, 0):
            copy.start()

    @pl.when(b < nu_ref[0])
    def _():
        o = bo_ref[b]
        slot = o % 2

        @pl.when((b == 0) | (o != bo_ref[jnp.maximum(b - 1, 0)]))
        def _():
            @pl.when(o + 1 < ne_ref[0])
            def _():
                for copy in fetch(o + 1, 1 - slot):
                    copy.start()

            for copy, (_, w_f, w_s) in zip(fetch(o, slot), weights):
                copy.wait()
                w_s[...] = w_f[slot].astype(BF16)

        x = jnp.concatenate([c.astype(BF16) for c in _load_slabs(x_ref, 0, slab_scr, rows, nch)], axis=1)
        hg = jnp.dot(x, wg_s[...], preferred_element_type=F32)
        hu = jnp.dot(x, wu_s[...], preferred_element_type=F32)
        act = (jax.nn.silu(hg) * hu).astype(BF16)
        y = jnp.dot(act, wd_s[...], preferred_element_type=F32)
        _store_slabs(y_ref, slab_scr, y)

    @pl.when(b >= nu_ref[0])
    def _():
        y_ref[...] = jnp.zeros_like(y_ref)


def _experts(blk_ord, exp_list, n_used, n_exp_used, xs, wg, wu, wd, rows):
    E, D, FF = wg.shape
    nch = D // LANES
    nblk = xs.shape[0] // (rows * nch)
    xmap = lambda b, bo, el, nu, ne: (jnp.maximum(jnp.minimum(b, nu[0] - 1), 0), 0)
    return pl.pallas_call(
        functools.partial(_experts_kernel, rows=rows, nch=nch),
        grid_spec=pltpu.PrefetchScalarGridSpec(
            num_scalar_prefetch=4,
            grid=(nblk,),
            in_specs=[
                pl.BlockSpec((rows * nch, LANES), xmap),
                pl.BlockSpec(memory_space=pl.ANY),
                pl.BlockSpec(memory_space=pl.ANY),
                pl.BlockSpec(memory_space=pl.ANY),
            ],
            out_specs=pl.BlockSpec((rows * nch, LANES), lambda b, bo, el, nu, ne: (b, 0)),
            scratch_shapes=[pltpu.VMEM((2, D, FF), F32), pltpu.VMEM((2, D, FF), F32), pltpu.VMEM((2, FF, D), F32),
                            pltpu.VMEM((D, FF), BF16), pltpu.VMEM((D, FF), BF16), pltpu.VMEM((FF, D), BF16),
                            pltpu.VMEM((2, rows * nch // 2, LANES), F32), pltpu.SemaphoreType.DMA((2, 3))],
        ),
        out_shape=jax.ShapeDtypeStruct(xs.shape, BF16),
        compiler_params=_cparams(1),
        name="experts",
    )(blk_ord, exp_list, n_used, n_exp_used, xs, wg, wu, wd)


def _combine_kernel(dest_ref, dest_next_ref, wtok_ref, h1_ref, vp_ref, wsg_ref, wsu_ref, wsd_ref, gf_ref,
                    ys_hbm, out_ref, buf, slab_scr, sems, *, tc, nch):
    i = pl.program_id(0)
    slot = i % 2

    def row_copy(src_row, s, dst_row):
        return pltpu.make_async_copy(ys_hbm.at[pl.ds(src_row, nch), :],
                                     buf.at[s].at[pl.ds(dst_row, nch), :], sems.at[s])

    def gather(idx_ref, s):
        def issue(t, c):
            for k in range(TOP_K):
                row_copy(pl.multiple_of(idx_ref[t * TOP_K + k] * nch, nch), s,
                         pl.multiple_of((k * tc + t) * nch, nch)).start(priority=k % 2)
            return c

        lax.fori_loop(0, tc, issue, 0)

    @pl.when(i == 0)
    def _():
        gather(dest_ref, 0)

    @pl.when(i + 1 < pl.num_programs(0))
    def _():
        gather(dest_next_ref, 1 - slot)

    x = jnp.concatenate([c.astype(BF16) for c in _load_slabs(vp_ref, 0, slab_scr, tc, nch)], axis=1)
    hg = jnp.dot(x, wsg_ref[...], preferred_element_type=F32)
    hu = jnp.dot(x, wsu_ref[...], preferred_element_type=F32)
    act = (jax.nn.silu(hg) * hu).astype(BF16)
    out_ref[...] = h1_ref[...] + jnp.dot(act, wsd_ref[...], preferred_element_type=F32)

    def drain(t, c):
        for k in range(TOP_K):
            row_copy(0, slot, 0).wait()
        return c

    lax.fori_loop(0, tc, drain, 0)

    rows = buf.at[slot]
    for k in range(TOP_K):
        wk = jnp.broadcast_to(wtok_ref[:, k:k + 1], (tc, LANES))
        _fill_slab_scratch(rows, k * tc * nch, slab_scr, tc, nch)
        for j in range(nch):
            out_ref[:, j * LANES:(j + 1) * LANES] += wk * _slab_chunk(slab_scr, j, tc, nch)
    out_ref[...] = _rms(out_ref[...], gf_ref[...])


def _combine(dest_flat, wtok, h1, vp, wsg, wsu, wsd, gf, ys, tc):
    T, D = h1.shape
    FF = wsg.shape[1]
    nch = D // LANES
    const = lambda i: (0, 0)
    n_steps = T // tc
    return pl.pallas_call(
        functools.partial(_combine_kernel, tc=tc, nch=nch),
        grid=(n_steps,),
        in_specs=[
            pl.BlockSpec((tc * TOP_K,), lambda i: (i,), memory_space=pltpu.SMEM),
            pl.BlockSpec((tc * TOP_K,), lambda i: (jnp.minimum(i + 1, n_steps - 1),), memory_space=pltpu.SMEM),
            pl.BlockSpec((tc, TOP_K), lambda i: (i, 0)),
            pl.BlockSpec((tc, D), lambda i: (i, 0)),
            pl.BlockSpec((tc * nch, LANES), lambda i: (i, 0)),
            pl.BlockSpec((D, FF), const),
            pl.BlockSpec((D, FF), const),
            pl.BlockSpec((FF, D), const),
            pl.BlockSpec((1, D), const),
            pl.BlockSpec(memory_space=pl.ANY),
        ],
        out_specs=pl.BlockSpec((tc, D), lambda i: (i, 0)),
        out_shape=jax.ShapeDtypeStruct((T, D), F32),
        scratch_shapes=[pltpu.VMEM((2, TOP_K * tc * nch, LANES), BF16),
                        pltpu.VMEM((2, tc * nch // 2, LANES), F32), pltpu.SemaphoreType.DMA((2,))],
        compiler_params=_cparams(1),
        name="combine",
    )(dest_flat, dest_flat, wtok, h1, vp, wsg, wsu, wsd, gf, ys)


def _layer(h, attn_norm_g, w_in, forget_bias, sinks, w_branch_swa, w_branch_fox, w_out, ffn_norm_g,
           w_router, router_bias, w_exp_gate, w_exp_up, w_exp_down, w_sh_gate, w_sh_up, w_sh_down, out_g):
    B, S, D = h.shape
    T = B * S
    E = w_router.shape[1]
    x2 = h.reshape(T, D)

    o = np.cumsum([0, SWA_Q_W, SWA_KV_W, SWA_KV_W, FOX_W, FOX_W, FOX_W, FORGET_W, D, D])
    aq, ak, av, fq, fk, fv, fg, gsw, gfx = [w_in[:, o[i]:o[i + 1]] for i in range(9)]

    def dup(w):
        w4 = w.reshape(D, SWA_KV_HEADS, 1, HEAD_DIM)
        return jnp.broadcast_to(w4, (D, SWA_KV_HEADS, 2, HEAD_DIM)).reshape(D, SWA_KV_EXP_W)

    w_main = jnp.concatenate([gsw, gfx, aq, dup(ak), dup(av), fq, fk, fv], axis=1).astype(BF16)
    spread = lambda a: jnp.pad(a[..., None], [(0, 0)] * a.ndim + [(0, FORGET_STRIDE - 1)]).reshape(*a.shape[:-1], LANES)
    w_f = spread(fg).astype(BF16)
    fb = spread(forget_bias.astype(F32).reshape(1, FORGET_W))

    z, f = _inproj(x2, attn_norm_g.reshape(1, D), w_main, w_f, tm=min(1024, T))
    ccol = _forget_cumsum(f, fb, B, S)
    y_swa = _swa(z, sinks.astype(F32), B, S, col0=2 * D)
    y_fox = _fox(z, ccol, B, S, col0=2 * D + SWA_Q_W + 2 * SWA_KV_EXP_W)

    h1, vp, eidx, rank, wgt, cnt = _post(
        y_swa, y_fox, z, x2, w_branch_swa.astype(BF16), w_branch_fox.astype(BF16), w_out.astype(BF16),
        ffn_norm_g.reshape(1, D), w_router.T, router_bias.reshape(E, 1).astype(F32), tm=min(256, T))

    rows = MOE_ROWS
    n_blocks = T * TOP_K // rows + E
    counts = cnt[:, 0].astype(I32)
    padded = (counts + rows - 1) // rows * rows
    pad_end = jnp.cumsum(padded)
    pad_start = pad_end - padded
    experts = jnp.arange(E, dtype=I32)[:, None, None]
    start_of = jnp.sum(jnp.where(eidx[None] == experts, pad_start[:, None, None], 0), axis=0)
    dest_flat = (start_of + rank).T.reshape(T * TOP_K)
    n_used = (pad_end[-1] // rows).astype(I32)
    blk = jnp.arange(n_blocks, dtype=I32)
    blk_exp = jnp.minimum(jnp.sum((pad_end[None, :] <= blk[:, None] * rows).astype(I32), axis=1), E - 1)
    e_ids = jnp.arange(E, dtype=I32)
    owns = counts > 0
    ord_of = jnp.cumsum(owns.astype(I32)) - 1
    n_exp_used = jnp.sum(owns.astype(I32))
    exp_list = jnp.sum(jnp.where((ord_of[None, :] == e_ids[:, None]) & owns[None, :], e_ids[None, :], 0), axis=1)
    blk_ord = jnp.sum(jnp.where(blk_exp[:, None] == e_ids[None, :], ord_of[None, :], 0), axis=1)

    last_of_ord = jnp.sum(jnp.where((ord_of[None, :] == e_ids[:, None]) & owns[None, :],
                                    (pad_end // rows - 1)[None, :], 0), axis=1)
    zi = jnp.arange(2 * E, dtype=I32)
    zero_blocks = jnp.where(zi < n_exp_used, jnp.concatenate([last_of_ord, last_of_ord]),
                            n_used + zi - n_exp_used)
    n_zero = n_exp_used + n_blocks - n_used

    xs = _dispatch(zero_blocks, n_zero.reshape(1), dest_flat, vp, n_blocks * rows, td=min(256, T), blk_rows=rows)
    ys = _experts(blk_ord, exp_list, n_used.reshape(1), n_exp_used.reshape(1), xs,
                  w_exp_gate, w_exp_up, w_exp_down, rows)
    out = _combine(dest_flat, wgt.T, h1, vp, w_sh_gate.astype(BF16), w_sh_up.astype(BF16),
                   w_sh_down.astype(BF16), out_g.reshape(1, D), ys, tc=min(256, T))
    return out.reshape(B, S, D)


def kernel(x, attn_norm_g, w_in, forget_bias, sinks, w_branch_swa, w_branch_fox, w_out, ffn_norm_g,
           w_router, router_bias, w_exp_gate, w_exp_up, w_exp_down, w_sh_gate, w_sh_up, w_sh_down,
           final_norm_g):
    depth = w_in.shape[0]
    assert depth == 1, "the final norm is fused into the last layer's combine"
    return _layer(x, attn_norm_g[0], w_in[0], forget_bias[0], sinks[0], w_branch_swa[0], w_branch_fox[0],
                  w_out[0], ffn_norm_g[0], w_router[0], router_bias[0], w_exp_gate[0], w_exp_up[0],
                  w_exp_down[0], w_sh_gate[0], w_sh_up[0], w_sh_down[0], final_norm_g)
```

```python
import functools

import numpy as np
import jax
import jax.numpy as jnp
from jax import lax
from jax.experimental import pallas as pl
from jax.experimental.pallas import tpu as pltpu

F32 = jnp.float32
BF16 = jnp.bfloat16
I32 = jnp.int32

HEAD_DIM = 64
SWA_Q_HEADS = 16
SWA_KV_HEADS = 4
FOX_HEADS = 16
WINDOW = 128
BLOCK = 128
N_GROUPS = 8
TOPK_GROUPS = 4
TOP_K = 8
ROUTED_SCALE = 2.5
RMS_EPS = 1e-6
LANES = 128
SWA_Q_W = SWA_Q_HEADS * HEAD_DIM
SWA_KV_W = SWA_KV_HEADS * HEAD_DIM
FOX_W = FOX_HEADS * HEAD_DIM
FORGET_W = FOX_HEADS
FORGET_STRIDE = 8
SWA_KV_EXP_W = 2 * SWA_KV_W
Z_TILE = 1024
FOX_TK = 256
FOX_PAIRS_PER_LOOP = 8
MOE_ROWS = 256
VMEM_LIMIT = 56 * 1024 * 1024
HI = lax.Precision.HIGHEST
NEG_INF = float("-inf")


def _cparams(n_axes):
    return pltpu.CompilerParams(dimension_semantics=("arbitrary",) * n_axes,
                                vmem_limit_bytes=VMEM_LIMIT)


def _rms(x, g):
    return x * lax.rsqrt(jnp.mean(x * x, axis=-1, keepdims=True) + RMS_EPS) * g


def _store_slabs(ref, scr, x):
    rows = x.shape[0]
    n = x.shape[1] // LANES
    hn = n // 2
    for j in range(n):
        scr[j // hn, pl.ds(j % hn, rows, stride=hn), :] = x[:, j * LANES:(j + 1) * LANES]
    halves = [scr[i].reshape(rows, 1, hn, LANES) for i in range(2)]
    ref[...] = jnp.concatenate(halves, axis=1).reshape(rows * n, LANES).astype(BF16)


def _fill_slab_scratch(ref, base, scr, rows, n):
    hn = n // 2
    xf = ref[pl.ds(base, rows * n), :].astype(F32).reshape(rows, 2, hn, LANES)
    for i in range(2):
        scr[i] = xf[:, i].reshape(rows * hn, LANES)


def _slab_chunk(scr, j, rows, n):
    hn = n // 2
    return scr[j // hn, pl.ds(j % hn, rows, stride=hn), :]


def _load_slabs(ref, base, scr, rows, n):
    _fill_slab_scratch(ref, base, scr, rows, n)
    return [_slab_chunk(scr, j, rows, n) for j in range(n)]


def _inproj_kernel(x_ref, g_ref, w_ref, wf_ref, z_ref, f_ref, u_scr, *, n_gate_tiles):
    j = pl.program_id(1)

    @pl.when(j == 0)
    def _():
        u = _rms(x_ref[...], g_ref[...]).astype(BF16)
        u_scr[...] = u
        f_ref[...] = jnp.dot(u, wf_ref[...], preferred_element_type=F32)

    z = jnp.dot(u_scr[...], w_ref[...], preferred_element_type=F32)

    @pl.when(j < n_gate_tiles)
    def _():
        z_ref[...] = jax.nn.sigmoid(z).astype(BF16)

    @pl.when(j >= n_gate_tiles)
    def _():
        z_ref[...] = z.astype(BF16)


def _inproj(x2, g, w_main, w_f, tm):
    T, D = x2.shape
    ZW = w_main.shape[1]
    return pl.pallas_call(
        functools.partial(_inproj_kernel, n_gate_tiles=2 * D // Z_TILE),
        grid=(T // tm, ZW // Z_TILE),
        in_specs=[
            pl.BlockSpec((tm, D), lambda i, j: (i, 0)),
            pl.BlockSpec((1, D), lambda i, j: (0, 0)),
            pl.BlockSpec((D, Z_TILE), lambda i, j: (0, j)),
            pl.BlockSpec((D, LANES), lambda i, j: (0, 0)),
        ],
        out_specs=[
            pl.BlockSpec((tm, Z_TILE), lambda i, j: (i, j)),
            pl.BlockSpec((tm, LANES), lambda i, j: (i, 0)),
        ],
        out_shape=[jax.ShapeDtypeStruct((T, ZW), BF16), jax.ShapeDtypeStruct((T, LANES), F32)],
        scratch_shapes=[pltpu.VMEM((tm, D), BF16)],
        compiler_params=_cparams(2),
        name="inproj",
    )(x2, g, w_main, w_f)


def _fcum_kernel(f_ref, b_ref, ccol_ref, carry_ref):
    @pl.when(pl.program_id(1) == 0)
    def _():
        carry_ref[...] = jnp.zeros_like(carry_ref)

    a = f_ref[...] + b_ref[...]
    ls = jnp.minimum(a, 0.0) - jnp.log1p(jnp.exp(-jnp.abs(a)))
    n = ls.shape[0]
    tri = (lax.broadcasted_iota(I32, (n, n), 1) <= lax.broadcasted_iota(I32, (n, n), 0)).astype(F32)
    c = jnp.dot(tri, ls, precision=HI, preferred_element_type=F32) + carry_ref[...]
    ccol_ref[...] = c
    carry_ref[...] = c[n - 1:n, :]


def _forget_cumsum(f, bias, B, S):
    T = B * S
    nb = S // FOX_TK
    return pl.pallas_call(
        _fcum_kernel,
        grid=(B, nb),
        in_specs=[
            pl.BlockSpec((FOX_TK, LANES), lambda b, j: (b * nb + j, 0)),
            pl.BlockSpec((1, LANES), lambda b, j: (0, 0)),
        ],
        out_specs=pl.BlockSpec((FOX_TK, LANES), lambda b, j: (b * nb + j, 0)),
        out_shape=jax.ShapeDtypeStruct((T, LANES), F32),
        scratch_shapes=[pltpu.VMEM((1, LANES), F32)],
        compiler_params=_cparams(2),
        name="forget_cumsum",
    )(f, bias)


def _swa_kernel(sink_ref, q_ref, kp_ref, kc_ref, vp_ref, vc_ref, y_ref, *, slopes):
    qi = pl.program_id(1)
    kk = jnp.concatenate([kp_ref[...], kc_ref[...]], axis=0)
    vv = jnp.concatenate([vp_ref[...], vc_ref[...]], axis=0)
    row = lax.broadcasted_iota(I32, (BLOCK, 2 * BLOCK), 0)
    col = lax.broadcasted_iota(I32, (BLOCK, 2 * BLOCK), 1)
    dist = row + BLOCK - col
    valid = (dist >= 0) & (dist < WINDOW) & ((qi > 0) | (col >= BLOCK))
    distf = dist.astype(F32)
    lo_half = lax.broadcasted_iota(I32, (BLOCK, LANES), 1) < HEAD_DIM
    group = SWA_Q_HEADS // SWA_KV_HEADS
    for pair in range(SWA_Q_HEADS // 2):
        q2 = q_ref[:, pair * LANES:(pair + 1) * LANES]
        g = (2 * pair) // group
        kg = kk[:, g * LANES:(g + 1) * LANES]
        vg = vv[:, g * LANES:(g + 1) * LANES]
        outs = []
        for half in range(2):
            h = 2 * pair + half
            keep = lo_half if half == 0 else jnp.logical_not(lo_half)
            qm = jnp.where(keep, q2, jnp.zeros_like(q2)) * jnp.asarray(HEAD_DIM ** -0.5, BF16)
            s = lax.dot_general(qm, kg, (((1,), (1,)), ((), ())), preferred_element_type=F32)
            s = jnp.where(valid, s - slopes[h] * distf, NEG_INF)
            sink = sink_ref[h]
            m = jnp.maximum(jnp.max(s, axis=-1, keepdims=True), sink)
            p = jnp.exp(s - m)
            denom = jnp.sum(p, axis=-1, keepdims=True) + jnp.exp(sink - m)
            o = jnp.dot(p.astype(BF16), vg, preferred_element_type=F32)
            outs.append(o / denom)
        y_ref[:, pair * LANES:(pair + 1) * LANES] = jnp.where(lo_half, outs[0], outs[1]).astype(BF16)


def _swa(z, sinks, B, S, col0):
    T = B * S
    nq = S // BLOCK
    qb = col0 // SWA_Q_W
    kb = (col0 + SWA_Q_W) // SWA_KV_EXP_W
    vb = kb + 1
    slopes = tuple(2.0 ** (-8.0 * (h + 1) / SWA_Q_HEADS) for h in range(SWA_Q_HEADS))
    cur = lambda b, i: b * nq + i
    prev = lambda b, i: jnp.maximum(b * nq + i - 1, 0)
    return pl.pallas_call(
        functools.partial(_swa_kernel, slopes=slopes),
        grid=(B, nq),
        in_specs=[
            pl.BlockSpec(memory_space=pltpu.SMEM),
            pl.BlockSpec((BLOCK, SWA_Q_W), lambda b, i: (cur(b, i), qb)),
            pl.BlockSpec((BLOCK, SWA_KV_EXP_W), lambda b, i: (prev(b, i), kb)),
            pl.BlockSpec((BLOCK, SWA_KV_EXP_W), lambda b, i: (cur(b, i), kb)),
            pl.BlockSpec((BLOCK, SWA_KV_EXP_W), lambda b, i: (prev(b, i), vb)),
            pl.BlockSpec((BLOCK, SWA_KV_EXP_W), lambda b, i: (cur(b, i), vb)),
        ],
        out_specs=pl.BlockSpec((BLOCK, SWA_Q_W), lambda b, i: (cur(b, i), 0)),
        out_shape=jax.ShapeDtypeStruct((T, SWA_Q_W), BF16),
        compiler_params=_cparams(2),
        name="swa",
    )(sinks, z, z, z, z, z)


def _split3(x):
    hi = x.astype(BF16).astype(F32)
    r1 = x - hi
    mid = r1.astype(BF16).astype(F32)
    lo = (r1 - mid).astype(BF16).astype(F32)
    return hi, mid, lo


def _split3_adjacent(c, lane):
    hi, mid, lo = _split3(jnp.where((lane & (FORGET_STRIDE - 1)) == 0, c, 0.0))
    return hi + pltpu.roll(mid, 1, axis=1) + pltpu.roll(lo, 2, axis=1)


def _aug_lanes(lane, terms, h, val0, one0):
    ones = jnp.where((lane >= one0) & (lane < one0 + 3), 1.0, 0.0)
    moved = pltpu.roll(terms, (val0 - h * FORGET_STRIDE) % LANES, axis=1)
    return jnp.where((lane >= val0) & (lane < val0 + 3), moved, ones)


def _fox_kernel(q_ref, k_ref, v_ref, c_ref, y_ref, ka_ref, va_ref):
    qi = pl.program_id(1)
    S = k_ref.shape[0]
    lane_k = lax.broadcasted_iota(I32, (FOX_TK, LANES), 1)
    lane_q = lax.broadcasted_iota(I32, (BLOCK, LANES), 1)

    def head_lanes(h):
        return slice(h * LANES, (h + 1) * LANES)

    def spare0(h):
        return HEAD_DIM if h % 2 == 0 else 0

    def keep(lane, h):
        return (lane < HEAD_DIM) if h % 2 == 0 else (lane >= HEAD_DIM)

    @pl.when(qi == 0)
    def _():
        def build(r, carry):
            r0 = pl.multiple_of(r * FOX_TK, FOX_TK)
            parts = _split3_adjacent(-c_ref[pl.ds(r0, FOX_TK), :], lane_k)
            for pair in range(FOX_HEADS // 2):
                kp = k_ref[pl.ds(r0, FOX_TK), pair * LANES:(pair + 1) * LANES].astype(F32)
                vp = v_ref[pl.ds(r0, FOX_TK), pair * LANES:(pair + 1) * LANES]
                for h in (2 * pair, 2 * pair + 1):
                    extra = _aug_lanes(lane_k, parts, h, spare0(h), spare0(h) + 3)
                    ka_ref[pl.ds(r0, FOX_TK), head_lanes(h)] = jnp.where(keep(lane_k, h), kp, extra).astype(BF16)
                    va_ref[pl.ds(r0, FOX_TK), head_lanes(h)] = jnp.where(keep(lane_k, h), vp, jnp.ones_like(vp))
            return carry

        lax.fori_loop(0, S // FOX_TK, build, 0)

    n_full = qi // (FOX_TK // BLOCK)
    row = qi * BLOCK + lax.broadcasted_iota(I32, (BLOCK, FOX_TK), 0)
    col = lax.broadcasted_iota(I32, (BLOCK, FOX_TK), 1)
    ct_parts = _split3_adjacent(c_ref[pl.ds(pl.multiple_of(qi * BLOCK, BLOCK), BLOCK), :], lane_q)

    hpl = 2 * FOX_PAIRS_PER_LOOP
    for h0 in range(0, FOX_HEADS, hpl):
        heads = list(range(h0, h0 + hpl))
        qas = []
        for h in heads:
            q2 = q_ref[:, (h // 2) * LANES:(h // 2 + 1) * LANES].astype(F32) * (HEAD_DIM ** -0.5)
            extra = _aug_lanes(lane_q, ct_parts, h, spare0(h) + 3, spare0(h))
            qas.append(jnp.where(keep(lane_q, h), q2, extra).astype(BF16))

        def step(j, carry, masked, heads=heads, qas=qas):
            k0 = pl.multiple_of(j * FOX_TK, FOX_TK)
            out = []
            for i, h in enumerate(heads):
                m, acc = carry[2 * i:2 * i + 2]
                s = lax.dot_general(qas[i], ka_ref[pl.ds(k0, FOX_TK), head_lanes(h)],
                                    (((1,), (1,)), ((), ())), preferred_element_type=F32)
                if masked:
                    s = jnp.where(col + j * FOX_TK <= row, s, NEG_INF)
                m_new = jnp.maximum(m, jnp.max(s, axis=-1, keepdims=True))
                p = jnp.exp(s - m_new).astype(BF16)
                acc = jnp.exp(m - m_new) * acc + jnp.dot(p, va_ref[pl.ds(k0, FOX_TK), head_lanes(h)],
                                                         preferred_element_type=F32)
                out += [m_new, acc]
            return tuple(out)

        init = (jnp.full((BLOCK, 1), -1e30, F32), jnp.zeros((BLOCK, LANES), F32)) * hpl
        carry = lax.fori_loop(0, n_full, functools.partial(step, masked=False), init)
        carry = step(n_full, carry, True)
        for i in range(0, hpl, 2):
            acc_lo, acc_hi = carry[2 * i + 1], carry[2 * i + 3]
            o_lo = acc_lo / acc_lo[:, LANES - 1:LANES]
            o_hi = acc_hi / acc_hi[:, 0:1]
            pair = heads[i] // 2
            y_ref[:, pair * LANES:(pair + 1) * LANES] = jnp.where(lane_q < HEAD_DIM, o_lo, o_hi).astype(BF16)


def _fox(z, ccol, B, S, col0):
    T = B * S
    nq = S // BLOCK
    qb = col0 // FOX_W
    return pl.pallas_call(
        _fox_kernel,
        grid=(B, nq),
        in_specs=[
            pl.BlockSpec((BLOCK, FOX_W), lambda b, i: (b * nq + i, qb)),
            pl.BlockSpec((S, FOX_W), lambda b, i: (b, qb + 1)),
            pl.BlockSpec((S, FOX_W), lambda b, i: (b, qb + 2)),
            pl.BlockSpec((S, LANES), lambda b, i: (b, 0)),
        ],
        out_specs=pl.BlockSpec((BLOCK, FOX_W), lambda b, i: (b * nq + i, 0)),
        out_shape=jax.ShapeDtypeStruct((T, FOX_W), BF16),
        scratch_shapes=[pltpu.VMEM((S, FOX_HEADS * LANES), BF16), pltpu.VMEM((S, FOX_HEADS * LANES), BF16)],
        compiler_params=_cparams(2),
        name="fox",
    )(z, z, z, ccol)


def _first_argmax(vals, iota, n):
    m = jnp.max(vals, axis=0, keepdims=True)
    idx = jnp.min(jnp.where(vals == m, iota, n), axis=0, keepdims=True)
    return m, idx


def _route(logits_t, rb):
    E, tm = logits_t.shape
    gs = E // N_GROUPS
    scores = jax.nn.sigmoid(logits_t)
    biased = scores + rb
    iota_g = lax.broadcasted_iota(I32, (gs, tm), 0)
    iota_n = lax.broadcasted_iota(I32, (N_GROUPS, tm), 0)
    grp = jnp.zeros((N_GROUPS, tm), F32)
    for g in range(N_GROUPS):
        blk = biased[g * gs:(g + 1) * gs, :]
        m1, i1 = _first_argmax(blk, iota_g, gs)
        m2 = jnp.max(jnp.where(iota_g == i1, NEG_INF, blk), axis=0, keepdims=True)
        grp = jnp.where(iota_n == g, m1 + m2, grp)
    gsel = jnp.zeros((N_GROUPS, tm), jnp.bool_)
    cur = grp
    for _ in range(TOPK_GROUPS):
        _, gi = _first_argmax(cur, iota_n, N_GROUPS)
        oh = iota_n == gi
        gsel = gsel | oh
        cur = jnp.where(oh, NEG_INF, cur)
    gself = gsel.astype(F32)
    emask = jnp.concatenate(
        [jnp.broadcast_to(gself[g:g + 1, :], (gs, tm)) for g in range(N_GROUPS)], axis=0) > 0.5
    cur = jnp.where(emask, biased, NEG_INF)
    iota_e = lax.broadcasted_iota(I32, (E, tm), 0)
    sel = jnp.zeros((E, tm), jnp.bool_)
    picks = []
    for _ in range(TOP_K):
        _, ei = _first_argmax(cur, iota_e, E)
        oh = iota_e == ei
        sel = sel | oh
        cur = jnp.where(oh, NEG_INF, cur)
        picks.append(ei)
    w = jnp.where(sel, scores, 0.0)
    w = w / jnp.sum(w, axis=0, keepdims=True) * ROUTED_SCALE
    return sel, w, picks, iota_e


def _post_kernel(ysw_ref, yfx_ref, gsw_ref, gfx_ref, x_ref, wsw_ref, wfx_ref, wo_ref, g2_ref,
                 wrt_ref, rb_ref, h1_ref, vp_ref, eidx_ref, rank_ref, wgt_ref, cnt_ref, carry_ref, slab_scr):
    @pl.when(pl.program_id(0) == 0)
    def _():
        carry_ref[...] = jnp.zeros_like(carry_ref)

    tm = x_ref.shape[0]
    a = jnp.dot(ysw_ref[...], wsw_ref[...], preferred_element_type=F32)
    b = jnp.dot(yfx_ref[...], wfx_ref[...], preferred_element_type=F32)
    mix = gsw_ref[...].astype(F32) * a + gfx_ref[...].astype(F32) * b
    h1 = x_ref[...] + jnp.dot(mix.astype(BF16), wo_ref[...], preferred_element_type=F32)
    h1_ref[...] = h1
    v = _rms(h1, g2_ref[...])
    _store_slabs(vp_ref, slab_scr, v)

    logits_t = lax.dot_general(wrt_ref[...], v, (((1,), (1,)), ((), ())),
                               precision=HI, preferred_element_type=F32)
    sel, w, picks, iota_e = _route(logits_t, rb_ref[...])

    before = (lax.broadcasted_iota(I32, (tm, tm), 0) < lax.broadcasted_iota(I32, (tm, tm), 1)).astype(BF16)
    rank = jnp.dot(sel.astype(BF16), before, preferred_element_type=F32) + carry_ref[...]
    carry_ref[...] = carry_ref[...] + jnp.sum(sel.astype(F32), axis=1, keepdims=True)
    cnt_ref[...] = carry_ref[...]

    iota_k = lax.broadcasted_iota(I32, (TOP_K, tm), 0)
    eidx = jnp.zeros((TOP_K, tm), I32)
    rnk = jnp.zeros((TOP_K, tm), F32)
    wgt = jnp.zeros((TOP_K, tm), F32)
    for k, ei in enumerate(picks):
        oh = iota_e == ei
        eidx = jnp.where(iota_k == k, ei, eidx)
        rnk = jnp.where(iota_k == k, jnp.sum(jnp.where(oh, rank, 0.0), axis=0, keepdims=True), rnk)
        wgt = jnp.where(iota_k == k, jnp.sum(jnp.where(oh, w, 0.0), axis=0, keepdims=True), wgt)
    eidx_ref[...] = eidx
    rank_ref[...] = rnk.astype(I32)
    wgt_ref[...] = wgt


def _post(ysw, yfx, z, x2, wsw, wfx, wo, g2, wrt, rb, tm):
    T, D = x2.shape
    E = wrt.shape[0]
    nch = D // LANES
    const = lambda i: (0, 0)
    return pl.pallas_call(
        _post_kernel,
        grid=(T // tm,),
        in_specs=[
            pl.BlockSpec((tm, SWA_Q_W), lambda i: (i, 0)),
            pl.BlockSpec((tm, FOX_W), lambda i: (i, 0)),
            pl.BlockSpec((tm, D), lambda i: (i, 0)),
            pl.BlockSpec((tm, D), lambda i: (i, 1)),
            pl.BlockSpec((tm, D), lambda i: (i, 0)),
            pl.BlockSpec((SWA_Q_W, D), const),
            pl.BlockSpec((FOX_W, D), const),
            pl.BlockSpec((D, D), const),
            pl.BlockSpec((1, D), const),
            pl.BlockSpec((E, D), const),
            pl.BlockSpec((E, 1), const),
        ],
        out_specs=[
            pl.BlockSpec((tm, D), lambda i: (i, 0)),
            pl.BlockSpec((tm * nch, LANES), lambda i: (i, 0)),
            pl.BlockSpec((TOP_K, tm), lambda i: (0, i)),
            pl.BlockSpec((TOP_K, tm), lambda i: (0, i)),
            pl.BlockSpec((TOP_K, tm), lambda i: (0, i)),
            pl.BlockSpec((E, 1), const),
        ],
        out_shape=[
            jax.ShapeDtypeStruct((T, D), F32),
            jax.ShapeDtypeStruct((T * nch, LANES), BF16),
            jax.ShapeDtypeStruct((TOP_K, T), I32),
            jax.ShapeDtypeStruct((TOP_K, T), I32),
            jax.ShapeDtypeStruct((TOP_K, T), F32),
            jax.ShapeDtypeStruct((E, 1), F32),
        ],
        scratch_shapes=[pltpu.VMEM((E, 1), F32), pltpu.VMEM((2, tm * nch // 2, LANES), F32)],
        compiler_params=_cparams(1),
        name="post_mixer",
    )(ysw, yfx, z, z, x2, wsw, wfx, wo, g2, wrt, rb)


def _dispatch_kernel(zblk_ref, nz_ref, dest_ref, vp_ref, xs_out, zeros, sem, zsem, *, td, nch, blk_rows):
    @pl.when(pl.program_id(0) == 0)
    def _():
        zeros[...] = jnp.zeros_like(zeros)

        def zero_copy(blk):
            dst = pl.multiple_of(blk * blk_rows * nch, blk_rows * nch)
            return pltpu.make_async_copy(zeros, xs_out.at[pl.ds(dst, blk_rows * nch), :], zsem)

        def zstart(i, c):
            zero_copy(zblk_ref[i]).start()
            return c

        def zwait(i, c):
            zero_copy(0).wait()
            return c

        lax.fori_loop(0, nz_ref[0], zstart, 0)
        lax.fori_loop(0, nz_ref[0], zwait, 0)

    def row_copy(src_row, dst_row):
        return pltpu.make_async_copy(vp_ref.at[pl.ds(src_row, nch), :],
                                     xs_out.at[pl.ds(dst_row, nch), :], sem)

    def issue(t, c):
        src = pl.multiple_of(t * nch, nch)
        for k in range(TOP_K):
            row_copy(src, pl.multiple_of(dest_ref[t * TOP_K + k] * nch, nch)).start(priority=k % 2)
        return c

    lax.fori_loop(0, td, issue, 0)

    def drain(t, c):
        for k in range(TOP_K):
            row_copy(0, 0).wait()
        return c

    lax.fori_loop(0, td, drain, 0)


def _dispatch(zero_blocks, n_zero, dest_flat, vp, n_rows, td, blk_rows):
    nch = vp.shape[0] * TOP_K // dest_flat.shape[0]
    T = vp.shape[0] // nch
    return pl.pallas_call(
        functools.partial(_dispatch_kernel, td=td, nch=nch, blk_rows=blk_rows),
        grid_spec=pltpu.PrefetchScalarGridSpec(
            num_scalar_prefetch=2,
            grid=(T // td,),
            in_specs=[
                pl.BlockSpec((td * TOP_K,), lambda i, zb, nz: (i,), memory_space=pltpu.SMEM),
                pl.BlockSpec((td * nch, LANES), lambda i, zb, nz: (i, 0)),
            ],
            out_specs=pl.BlockSpec(memory_space=pl.ANY),
            scratch_shapes=[pltpu.VMEM((blk_rows * nch, LANES), BF16), pltpu.SemaphoreType.DMA(()),
                            pltpu.SemaphoreType.DMA(())],
        ),
        out_shape=jax.ShapeDtypeStruct((n_rows * nch, LANES), BF16),
        compiler_params=_cparams(1),
        name="dispatch",
    )(zero_blocks, n_zero, dest_flat, vp)


def _experts_kernel(bo_ref, el_ref, nu_ref, ne_ref, x_ref, wg_hbm, wu_hbm, wd_hbm, y_ref,
                    wg_f, wu_f, wd_f, wg_s, wu_s, wd_s, slab_scr, sems, *, rows, nch):
    b = pl.program_id(0)
    weights = ((wg_hbm, wg_f, wg_s), (wu_hbm, wu_f, wu_s), (wd_hbm, wd_f, wd_s))

    def fetch(o, slot):
        e = el_ref[o]
        return [pltpu.make_async_copy(w_hbm.at[e], w_f.at[slot], sems.at[slot, i])
                for i, (w_hbm, w_f, _) in enumerate(weights)]

    @pl.when(b == 0)
    def _():
        for copy in fetch(0, 0):
            copy.start()

    @pl.when(b < nu_ref[0])
    def _():
        o = bo_ref[b]
        slot = o % 2

        @pl.when((b == 0) | (o != bo_ref[jnp.maximum(b - 1, 0)]))
        def _():
            @pl.when(o + 1 < ne_ref[0])
            def _():
                for copy in fetch(o + 1, 1 - slot):
                    copy.start()

            for copy, (_, w_f, w_s) in zip(fetch(o, slot), weights):
                copy.wait()
                w_s[...] = w_f[slot].astype(BF16)

        x = jnp.concatenate([c.astype(BF16) for c in _load_slabs(x_ref, 0, slab_scr, rows, nch)], axis=1)
        hg = jnp.dot(x, wg_s[...], preferred_element_type=F32)
        hu = jnp.dot(x, wu_s[...], preferred_element_type=F32)
        act = (jax.nn.silu(hg) * hu).astype(BF16)
        y = jnp.dot(act, wd_s[...], preferred_element_type=F32)
        _store_slabs(y_ref, slab_scr, y)

    @pl.when(b >= nu_ref[0])
    def _():
        y_ref[...] = jnp.zeros_like(y_ref)


def _experts(blk_ord, exp_list, n_used, n_exp_used, xs, wg, wu, wd, rows):
    E, D, FF = wg.shape
    nch = D // LANES
    nblk = xs.shape[0] // (rows * nch)
    xmap = lambda b, bo, el, nu, ne: (jnp.maximum(jnp.minimum(b, nu[0] - 1), 0), 0)
    return pl.pallas_call(
        functools.partial(_experts_kernel, rows=rows, nch=nch),
        grid_spec=pltpu.PrefetchScalarGridSpec(
            num_scalar_prefetch=4,
            grid=(nblk,),
            in_specs=[
                pl.BlockSpec((rows * nch, LANES), xmap),
                pl.BlockSpec(memory_space=pl.ANY),
                pl.BlockSpec(memory_space=pl.ANY),
                pl.BlockSpec(memory_space=pl.ANY),
            ],
            out_specs=pl.BlockSpec((rows * nch, LANES), lambda b, bo, el, nu, ne: (b, 0)),
            scratch_shapes=[pltpu.VMEM((2, D, FF), F32), pltpu.VMEM((2, D, FF), F32), pltpu.VMEM((2, FF, D), F32),
                            pltpu.VMEM((D, FF), BF16), pltpu.VMEM((D, FF), BF16), pltpu.VMEM((FF, D), BF16),
                            pltpu.VMEM((2, rows * nch // 2, LANES), F32), pltpu.SemaphoreType.DMA((2, 3))],
        ),
        out_shape=jax.ShapeDtypeStruct(xs.shape, BF16),
        compiler_params=_cparams(1),
        name="experts",
    )(blk_ord, exp_list, n_used, n_exp_used, xs, wg, wu, wd)


def _combine_kernel(dest_ref, dest_next_ref, wtok_ref, h1_ref, vp_ref, wsg_ref, wsu_ref, wsd_ref, gf_ref,
                    ys_hbm, out_ref, buf, slab_scr, sems, *, tc, nch):
    i = pl.program_id(0)
    slot = i % 2

    def row_copy(src_row, s, dst_row):
        return pltpu.make_async_copy(ys_hbm.at[pl.ds(src_row, nch), :],
                                     buf.at[s].at[pl.ds(dst_row, nch), :], sems.at[s])

    def gather(idx_ref, s):
        def issue(t, c):
            for k in range(TOP_K):
                row_copy(pl.multiple_of(idx_ref[t * TOP_K + k] * nch, nch), s,
                         pl.multiple_of((k * tc + t) * nch, nch)).start(priority=k % 2)
            return c

        lax.fori_loop(0, tc, issue, 0)

    @pl.when(i == 0)
    def _():
        gather(dest_ref, 0)

    x = jnp.concatenate([c.astype(BF16) for c in _load_slabs(vp_ref, 0, slab_scr, tc, nch)], axis=1)
    hg = jnp.dot(x, wsg_ref[...], preferred_element_type=F32)
    hu = jnp.dot(x, wsu_ref[...], preferred_element_type=F32)
    act = (jax.nn.silu(hg) * hu).astype(BF16)
    out_ref[...] = h1_ref[...] + jnp.dot(act, wsd_ref[...], preferred_element_type=F32)

    def drain(s):
        def body(t, c):
            for k in range(TOP_K):
                row_copy(0, s, 0).wait()
            return c

        lax.fori_loop(0, tc, body, 0)

    drain(slot)

    rows = buf.at[slot]
    per_chunk = tc // nch
    for k in range(TOP_K):
        wk = jnp.broadcast_to(wtok_ref[:, k:k + 1], (tc, LANES))
        _fill_slab_scratch(rows, k * tc * nch, slab_scr, tc, nch)
        for j in range(nch):
            for a in range((k * nch + j) * per_chunk, (k * nch + j + 1) * per_chunk):
                t, kk = divmod(a, TOP_K)
                row_copy(pl.multiple_of(dest_next_ref[a] * nch, nch), 1 - slot,
                         (kk * tc + t) * nch).start(priority=a % 2)
            out_ref[:, j * LANES:(j + 1) * LANES] += wk * _slab_chunk(slab_scr, j, tc, nch)
    out_ref[...] = _rms(out_ref[...], gf_ref[...])

    @pl.when(i == pl.num_programs(0) - 1)
    def _():
        drain(1 - slot)


def _combine(dest_flat, wtok, h1, vp, wsg, wsu, wsd, gf, ys, tc):
    T, D = h1.shape
    FF = wsg.shape[1]
    nch = D // LANES
    const = lambda i: (0, 0)
    n_steps = T // tc
    return pl.pallas_call(
        functools.partial(_combine_kernel, tc=tc, nch=nch),
        grid=(n_steps,),
        in_specs=[
            pl.BlockSpec((tc * TOP_K,), lambda i: (i,), memory_space=pltpu.SMEM),
            pl.BlockSpec((tc * TOP_K,), lambda i: (jnp.minimum(i + 1, n_steps - 1),), memory_space=pltpu.SMEM),
            pl.BlockSpec((tc, TOP_K), lambda i: (i, 0)),
            pl.BlockSpec((tc, D), lambda i: (i, 0)),
            pl.BlockSpec((tc * nch, LANES), lambda i: (i, 0)),
            pl.BlockSpec((D, FF), const),
            pl.BlockSpec((D, FF), const),
            pl.BlockSpec((FF, D), const),
            pl.BlockSpec((1, D), const),
            pl.BlockSpec(memory_space=pl.ANY),
        ],
        out_specs=pl.BlockSpec((tc, D), lambda i: (i, 0)),
        out_shape=jax.ShapeDtypeStruct((T, D), F32),
        scratch_shapes=[pltpu.VMEM((2, TOP_K * tc * nch, LANES), BF16),
                        pltpu.VMEM((2, tc * nch // 2, LANES), F32), pltpu.SemaphoreType.DMA((2,))],
        compiler_params=_cparams(1),
        name="combine",
    )(dest_flat, dest_flat, wtok, h1, vp, wsg, wsu, wsd, gf, ys)


def _layer(h, attn_norm_g, w_in, forget_bias, sinks, w_branch_swa, w_branch_fox, w_out, ffn_norm_g,
           w_router, router_bias, w_exp_gate, w_exp_up, w_exp_down, w_sh_gate, w_sh_up, w_sh_down, out_g):
    B, S, D = h.shape
    T = B * S
    E = w_router.shape[1]
    x2 = h.reshape(T, D)

    o = np.cumsum([0, SWA_Q_W, SWA_KV_W, SWA_KV_W, FOX_W, FOX_W, FOX_W, FORGET_W, D, D])
    aq, ak, av, fq, fk, fv, fg, gsw, gfx = [w_in[:, o[i]:o[i + 1]] for i in range(9)]

    def dup(w):
        w4 = w.reshape(D, SWA_KV_HEADS, 1, HEAD_DIM)
        return jnp.broadcast_to(w4, (D, SWA_KV_HEADS, 2, HEAD_DIM)).reshape(D, SWA_KV_EXP_W)

    w_main = jnp.concatenate([gsw, gfx, aq, dup(ak), dup(av), fq, fk, fv], axis=1).astype(BF16)
    spread = lambda a: jnp.pad(a[..., None], [(0, 0)] * a.ndim + [(0, FORGET_STRIDE - 1)]).reshape(*a.shape[:-1], LANES)
    w_f = spread(fg).astype(BF16)
    fb = spread(forget_bias.astype(F32).reshape(1, FORGET_W))

    z, f = _inproj(x2, attn_norm_g.reshape(1, D), w_main, w_f, tm=min(1024, T))
    ccol = _forget_cumsum(f, fb, B, S)
    y_swa = _swa(z, sinks.astype(F32), B, S, col0=2 * D)
    y_fox = _fox(z, ccol, B, S, col0=2 * D + SWA_Q_W + 2 * SWA_KV_EXP_W)

    h1, vp, eidx, rank, wgt, cnt = _post(
        y_swa, y_fox, z, x2, w_branch_swa.astype(BF16), w_branch_fox.astype(BF16), w_out.astype(BF16),
        ffn_norm_g.reshape(1, D), w_router.T, router_bias.reshape(E, 1).astype(F32), tm=min(256, T))

    rows = MOE_ROWS
    n_blocks = T * TOP_K // rows + E
    counts = cnt[:, 0].astype(I32)
    padded = (counts + rows - 1) // rows * rows
    pad_end = jnp.cumsum(padded)
    pad_start = pad_end - padded
    experts = jnp.arange(E, dtype=I32)[:, None, None]
    start_of = jnp.sum(jnp.where(eidx[None] == experts, pad_start[:, None, None], 0), axis=0)
    dest_flat = (start_of + rank).T.reshape(T * TOP_K)
    n_used = (pad_end[-1] // rows).astype(I32)
    blk = jnp.arange(n_blocks, dtype=I32)
    blk_exp = jnp.minimum(jnp.sum((pad_end[None, :] <= blk[:, None] * rows).astype(I32), axis=1), E - 1)
    e_ids = jnp.arange(E, dtype=I32)
    owns = counts > 0
    ord_of = jnp.cumsum(owns.astype(I32)) - 1
    n_exp_used = jnp.sum(owns.astype(I32))
    exp_list = jnp.sum(jnp.where((ord_of[None, :] == e_ids[:, None]) & owns[None, :], e_ids[None, :], 0), axis=1)
    blk_ord = jnp.sum(jnp.where(blk_exp[:, None] == e_ids[None, :], ord_of[None, :], 0), axis=1)

    last_of_ord = jnp.sum(jnp.where((ord_of[None, :] == e_ids[:, None]) & owns[None, :],
                                    (pad_end // rows - 1)[None, :], 0), axis=1)
    zi = jnp.arange(2 * E, dtype=I32)
    zero_blocks = jnp.where(zi < n_exp_used, jnp.concatenate([last_of_ord, last_of_ord]),
                            n_used + zi - n_exp_used)
    n_zero = n_exp_used + n_blocks - n_used

    xs = _dispatch(zero_blocks, n_zero.reshape(1), dest_flat, vp, n_blocks * rows, td=min(256, T), blk_rows=rows)
    ys = _experts(blk_ord, exp_list, n_used.reshape(1), n_exp_used.reshape(1), xs,
                  w_exp_gate, w_exp_up, w_exp_down, rows)
    out = _combine(dest_flat, wgt.T, h1, vp, w_sh_gate.astype(BF16), w_sh_up.astype(BF16),
                   w_sh_down.astype(BF16), out_g.reshape(1, D), ys, tc=min(256, T))
    return out.reshape(B, S, D)


def kernel(x, attn_norm_g, w_in, forget_bias, sinks, w_branch_swa, w_branch_fox, w_out, ffn_norm_g,
           w_router, router_bias, w_exp_gate, w_exp_up, w_exp_down, w_sh_gate, w_sh_up, w_sh_down,
           final_norm_g):
    depth = w_in.shape[0]
    assert depth == 1, "the final norm is fused into the last layer's combine"
    return _layer(x, attn_norm_g[0], w_in[0], forget_bias[0], sinks[0], w_branch_swa[0], w_branch_fox[0],
                  w_out[0], ffn_norm_g[0], w_router[0], router_bias[0], w_exp_gate[0], w_exp_up[0],
                  w_exp_down[0], w_sh_gate[0], w_sh_up[0], w_sh_down[0], final_norm_g)
```

```python
import functools

import numpy as np
import jax
import jax.numpy as jnp
from jax import lax
from jax.experimental import pallas as pl
from jax.experimental.pallas import tpu as pltpu

F32 = jnp.float32
BF16 = jnp.bfloat16
I32 = jnp.int32

HEAD_DIM = 64
SWA_Q_HEADS = 16
SWA_KV_HEADS = 4
FOX_HEADS = 16
WINDOW = 128
BLOCK = 128
N_GROUPS = 8
TOPK_GROUPS = 4
TOP_K = 8
ROUTED_SCALE = 2.5
RMS_EPS = 1e-6
LANES = 128
SWA_Q_W = SWA_Q_HEADS * HEAD_DIM
SWA_KV_W = SWA_KV_HEADS * HEAD_DIM
FOX_W = FOX_HEADS * HEAD_DIM
FORGET_W = FOX_HEADS
FORGET_STRIDE = 8
SWA_KV_EXP_W = 2 * SWA_KV_W
Z_TILE = 1024
FOX_TQ = 128
FOX_TK = 256
FOX_PAIRS_PER_LOOP = 8
MOE_ROWS = 256
POST_COL_CHUNK = 512
WEIGHT_SLOTS = 3
VMEM_LIMIT = 56 * 1024 * 1024
HI = lax.Precision.HIGHEST
NEG_INF = float("-inf")


def _cparams(n_axes):
    return pltpu.CompilerParams(dimension_semantics=("arbitrary",) * n_axes,
                                vmem_limit_bytes=VMEM_LIMIT)


def _rms(x, g):
    return x * lax.rsqrt(jnp.mean(x * x, axis=-1, keepdims=True) + RMS_EPS) * g


def _store_slabs(ref, scr, x):
    rows = x.shape[0]
    n = x.shape[1] // LANES
    hn = n // 2
    for j in range(n):
        scr[j // hn, pl.ds(j % hn, rows, stride=hn), :] = x[:, j * LANES:(j + 1) * LANES]
    halves = [scr[i].reshape(rows, 1, hn, LANES) for i in range(2)]
    ref[...] = jnp.concatenate(halves, axis=1).reshape(rows * n, LANES).astype(BF16)


def _fill_slab_scratch(ref, base, scr, rows, n):
    hn = n // 2
    xf = ref[pl.ds(base, rows * n), :].astype(F32).reshape(rows, 2, hn, LANES)
    for i in range(2):
        scr[i] = xf[:, i].reshape(rows * hn, LANES)


def _slab_chunk(scr, j, rows, n):
    hn = n // 2
    return scr[j // hn, pl.ds(j % hn, rows, stride=hn), :]


def _load_slabs(ref, base, scr, rows, n):
    _fill_slab_scratch(ref, base, scr, rows, n)
    return [_slab_chunk(scr, j, rows, n) for j in range(n)]


def _inproj_kernel(x_ref, g_ref, w_ref, wf_ref, z_ref, f_ref, u_scr, *, n_gate_tiles):
    j = pl.program_id(1)

    @pl.when(j == 0)
    def _():
        u = _rms(x_ref[...], g_ref[...]).astype(BF16)
        u_scr[...] = u
        f_ref[...] = jnp.dot(u, wf_ref[...], preferred_element_type=F32)

    z = jnp.dot(u_scr[...], w_ref[...], preferred_element_type=F32)

    @pl.when(j < n_gate_tiles)
    def _():
        z_ref[...] = jax.nn.sigmoid(z).astype(BF16)

    @pl.when(j >= n_gate_tiles)
    def _():
        z_ref[...] = z.astype(BF16)


def _inproj(x2, g, w_main, w_f, tm):
    T, D = x2.shape
    ZW = w_main.shape[1]
    return pl.pallas_call(
        functools.partial(_inproj_kernel, n_gate_tiles=2 * D // Z_TILE),
        grid=(T // tm, ZW // Z_TILE),
        in_specs=[
            pl.BlockSpec((tm, D), lambda i, j: (i, 0)),
            pl.BlockSpec((1, D), lambda i, j: (0, 0)),
            pl.BlockSpec((D, Z_TILE), lambda i, j: (0, j)),
            pl.BlockSpec((D, LANES), lambda i, j: (0, 0)),
        ],
        out_specs=[
            pl.BlockSpec((tm, Z_TILE), lambda i, j: (i, j)),
            pl.BlockSpec((tm, LANES), lambda i, j: (i, 0)),
        ],
        out_shape=[jax.ShapeDtypeStruct((T, ZW), BF16), jax.ShapeDtypeStruct((T, LANES), F32)],
        scratch_shapes=[pltpu.VMEM((tm, D), BF16)],
        compiler_params=_cparams(2),
        name="inproj",
    )(x2, g, w_main, w_f)


def _fcum_kernel(f_ref, b_ref, ccol_ref, carry_ref):
    @pl.when(pl.program_id(1) == 0)
    def _():
        carry_ref[...] = jnp.zeros_like(carry_ref)

    a = f_ref[...] + b_ref[...]
    ls = jnp.minimum(a, 0.0) - jnp.log1p(jnp.exp(-jnp.abs(a)))
    n = ls.shape[0]
    tri = (lax.broadcasted_iota(I32, (n, n), 1) <= lax.broadcasted_iota(I32, (n, n), 0)).astype(F32)
    c = jnp.dot(tri, ls, precision=HI, preferred_element_type=F32) + carry_ref[...]
    ccol_ref[...] = c
    carry_ref[...] = c[n - 1:n, :]


def _forget_cumsum(f, bias, B, S):
    T = B * S
    nb = S // FOX_TK
    return pl.pallas_call(
        _fcum_kernel,
        grid=(B, nb),
        in_specs=[
            pl.BlockSpec((FOX_TK, LANES), lambda b, j: (b * nb + j, 0)),
            pl.BlockSpec((1, LANES), lambda b, j: (0, 0)),
        ],
        out_specs=pl.BlockSpec((FOX_TK, LANES), lambda b, j: (b * nb + j, 0)),
        out_shape=jax.ShapeDtypeStruct((T, LANES), F32),
        scratch_shapes=[pltpu.VMEM((1, LANES), F32)],
        compiler_params=_cparams(2),
        name="forget_cumsum",
    )(f, bias)


def _swa_kernel(sink_ref, q_ref, kp_ref, kc_ref, vp_ref, vc_ref, y_ref, *, slopes):
    qi = pl.program_id(1)
    kk = jnp.concatenate([kp_ref[...], kc_ref[...]], axis=0)
    vv = jnp.concatenate([vp_ref[...], vc_ref[...]], axis=0)
    row = lax.broadcasted_iota(I32, (BLOCK, 2 * BLOCK), 0)
    col = lax.broadcasted_iota(I32, (BLOCK, 2 * BLOCK), 1)
    dist = row + BLOCK - col
    valid = (dist >= 0) & (dist < WINDOW) & ((qi > 0) | (col >= BLOCK))
    distf = dist.astype(F32)
    lo_half = lax.broadcasted_iota(I32, (BLOCK, LANES), 1) < HEAD_DIM
    group = SWA_Q_HEADS // SWA_KV_HEADS
    for pair in range(SWA_Q_HEADS // 2):
        q2 = q_ref[:, pair * LANES:(pair + 1) * LANES]
        g = (2 * pair) // group
        kg = kk[:, g * LANES:(g + 1) * LANES]
        vg = vv[:, g * LANES:(g + 1) * LANES]
        outs = []
        for half in range(2):
            h = 2 * pair + half
            keep = lo_half if half == 0 else jnp.logical_not(lo_half)
            qm = jnp.where(keep, q2, jnp.zeros_like(q2)) * jnp.asarray(HEAD_DIM ** -0.5, BF16)
            s = lax.dot_general(qm, kg, (((1,), (1,)), ((), ())), preferred_element_type=F32)
            s = jnp.where(valid, s - slopes[h] * distf, NEG_INF)
            sink = sink_ref[h]
            m = jnp.maximum(jnp.max(s, axis=-1, keepdims=True), sink)
            p = jnp.exp(s - m)
            denom = jnp.sum(p, axis=-1, keepdims=True) + jnp.exp(sink - m)
            o = jnp.dot(p.astype(BF16), vg, preferred_element_type=F32)
            outs.append(o / denom)
        y_ref[:, pair * LANES:(pair + 1) * LANES] = jnp.where(lo_half, outs[0], outs[1]).astype(BF16)


def _swa(z, sinks, B, S, col0):
    T = B * S
    nq = S // BLOCK
    qb = col0 // SWA_Q_W
    kb = (col0 + SWA_Q_W) // SWA_KV_EXP_W
    vb = kb + 1
    slopes = tuple(2.0 ** (-8.0 * (h + 1) / SWA_Q_HEADS) for h in range(SWA_Q_HEADS))
    cur = lambda b, i: b * nq + i
    prev = lambda b, i: jnp.maximum(b * nq + i - 1, 0)
    return pl.pallas_call(
        functools.partial(_swa_kernel, slopes=slopes),
        grid=(B, nq),
        in_specs=[
            pl.BlockSpec(memory_space=pltpu.SMEM),
            pl.BlockSpec((BLOCK, SWA_Q_W), lambda b, i: (cur(b, i), qb)),
            pl.BlockSpec((BLOCK, SWA_KV_EXP_W), lambda b, i: (prev(b, i), kb)),
            pl.BlockSpec((BLOCK, SWA_KV_EXP_W), lambda b, i: (cur(b, i), kb)),
            pl.BlockSpec((BLOCK, SWA_KV_EXP_W), lambda b, i: (prev(b, i), vb)),
            pl.BlockSpec((BLOCK, SWA_KV_EXP_W), lambda b, i: (cur(b, i), vb)),
        ],
        out_specs=pl.BlockSpec((BLOCK, SWA_Q_W), lambda b, i: (cur(b, i), 0)),
        out_shape=jax.ShapeDtypeStruct((T, SWA_Q_W), BF16),
        compiler_params=_cparams(2),
        name="swa",
    )(sinks, z, z, z, z, z)


def _split3(x):
    hi = x.astype(BF16).astype(F32)
    r1 = x - hi
    mid = r1.astype(BF16).astype(F32)
    lo = (r1 - mid).astype(BF16).astype(F32)
    return hi, mid, lo


def _split3_adjacent(c, lane):
    hi, mid, lo = _split3(jnp.where((lane & (FORGET_STRIDE - 1)) == 0, c, 0.0))
    return hi + pltpu.roll(mid, 1, axis=1) + pltpu.roll(lo, 2, axis=1)


def _aug_lanes(lane, terms, h, val0, one0):
    ones = jnp.where((lane >= one0) & (lane < one0 + 3), 1.0, 0.0)
    moved = pltpu.roll(terms, (val0 - h * FORGET_STRIDE) % LANES, axis=1)
    return jnp.where((lane >= val0) & (lane < val0 + 3), moved, ones)


def _fox_kernel(q_ref, k_ref, v_ref, c_ref, y_ref, ka_ref, va_ref):
    qi = pl.program_id(1)
    S = k_ref.shape[0]
    lane_k = lax.broadcasted_iota(I32, (FOX_TK, LANES), 1)
    lane_q = lax.broadcasted_iota(I32, (FOX_TQ, LANES), 1)

    def head_lanes(h):
        return slice(h * LANES, (h + 1) * LANES)

    def spare0(h):
        return HEAD_DIM if h % 2 == 0 else 0

    def keep(lane, h):
        return (lane < HEAD_DIM) if h % 2 == 0 else (lane >= HEAD_DIM)

    @pl.when(qi == 0)
    def _():
        def build(r, carry):
            r0 = pl.multiple_of(r * FOX_TK, FOX_TK)
            parts = _split3_adjacent(-c_ref[pl.ds(r0, FOX_TK), :], lane_k)
            for pair in range(FOX_HEADS // 2):
                kp = k_ref[pl.ds(r0, FOX_TK), pair * LANES:(pair + 1) * LANES].astype(F32)
                vp = v_ref[pl.ds(r0, FOX_TK), pair * LANES:(pair + 1) * LANES]
                for h in (2 * pair, 2 * pair + 1):
                    extra = _aug_lanes(lane_k, parts, h, spare0(h), spare0(h) + 3)
                    ka_ref[pl.ds(r0, FOX_TK), head_lanes(h)] = jnp.where(keep(lane_k, h), kp, extra).astype(BF16)
                    va_ref[pl.ds(r0, FOX_TK), head_lanes(h)] = jnp.where(keep(lane_k, h), vp, jnp.ones_like(vp))
            return carry

        lax.fori_loop(0, S // FOX_TK, build, 0)

    n_full = qi * FOX_TQ // FOX_TK
    row = qi * FOX_TQ + lax.broadcasted_iota(I32, (FOX_TQ, FOX_TK), 0)
    col = lax.broadcasted_iota(I32, (FOX_TQ, FOX_TK), 1)
    ct_parts = _split3_adjacent(c_ref[pl.ds(pl.multiple_of(qi * FOX_TQ, FOX_TQ), FOX_TQ), :], lane_q)

    hpl = 2 * FOX_PAIRS_PER_LOOP
    for h0 in range(0, FOX_HEADS, hpl):
        heads = list(range(h0, h0 + hpl))
        qas = []
        for h in heads:
            q2 = q_ref[:, (h // 2) * LANES:(h // 2 + 1) * LANES].astype(F32) * (HEAD_DIM ** -0.5)
            extra = _aug_lanes(lane_q, ct_parts, h, spare0(h) + 3, spare0(h))
            qas.append(jnp.where(keep(lane_q, h), q2, extra).astype(BF16))

        def step(j, carry, masked, heads=heads, qas=qas):
            k0 = pl.multiple_of(j * FOX_TK, FOX_TK)
            out = []
            for i, h in enumerate(heads):
                m, acc = carry[2 * i:2 * i + 2]
                s = lax.dot_general(qas[i], ka_ref[pl.ds(k0, FOX_TK), head_lanes(h)],
                                    (((1,), (1,)), ((), ())), preferred_element_type=F32)
                if masked:
                    s = jnp.where(col + j * FOX_TK <= row, s, NEG_INF)
                m_new = jnp.maximum(m, jnp.max(s, axis=-1, keepdims=True))
                p = jnp.exp(s - m_new).astype(BF16)
                acc = jnp.exp(m - m_new) * acc + jnp.dot(p, va_ref[pl.ds(k0, FOX_TK), head_lanes(h)],
                                                         preferred_element_type=F32)
                out += [m_new, acc]
            return tuple(out)

        init = (jnp.full((FOX_TQ, 1), -1e30, F32), jnp.zeros((FOX_TQ, LANES), F32)) * hpl
        carry = lax.fori_loop(0, n_full, functools.partial(step, masked=False), init)
        carry = step(n_full, carry, True)
        for i in range(0, hpl, 2):
            acc_lo, acc_hi = carry[2 * i + 1], carry[2 * i + 3]
            o_lo = acc_lo / acc_lo[:, LANES - 1:LANES]
            o_hi = acc_hi / acc_hi[:, 0:1]
            pair = heads[i] // 2
            y_ref[:, pair * LANES:(pair + 1) * LANES] = jnp.where(lane_q < HEAD_DIM, o_lo, o_hi).astype(BF16)


def _fox(z, ccol, B, S, col0):
    T = B * S
    nq = S // FOX_TQ
    qb = col0 // FOX_W
    return pl.pallas_call(
        _fox_kernel,
        grid=(B, nq),
        in_specs=[
            pl.BlockSpec((FOX_TQ, FOX_W), lambda b, i: (b * nq + i, qb)),
            pl.BlockSpec((S, FOX_W), lambda b, i: (b, qb + 1)),
            pl.BlockSpec((S, FOX_W), lambda b, i: (b, qb + 2)),
            pl.BlockSpec((S, LANES), lambda b, i: (b, 0)),
        ],
        out_specs=pl.BlockSpec((FOX_TQ, FOX_W), lambda b, i: (b * nq + i, 0)),
        out_shape=jax.ShapeDtypeStruct((T, FOX_W), BF16),
        scratch_shapes=[pltpu.VMEM((S, FOX_HEADS * LANES), BF16), pltpu.VMEM((S, FOX_HEADS * LANES), BF16)],
        compiler_params=_cparams(2),
        name="fox",
    )(z, z, z, ccol)


def _first_argmax(vals, iota, n):
    m = jnp.max(vals, axis=0, keepdims=True)
    idx = jnp.min(jnp.where(vals == m, iota, n), axis=0, keepdims=True)
    return m, idx


def _route(logits_t, rb, tick=lambda: None):
    E, tm = logits_t.shape
    gs = E // N_GROUPS
    scores = jax.nn.sigmoid(logits_t)
    biased = scores + rb
    iota_g = lax.broadcasted_iota(I32, (gs, tm), 0)
    iota_n = lax.broadcasted_iota(I32, (N_GROUPS, tm), 0)
    grp = jnp.zeros((N_GROUPS, tm), F32)
    for g in range(N_GROUPS):
        blk = biased[g * gs:(g + 1) * gs, :]
        m1, i1 = _first_argmax(blk, iota_g, gs)
        m2 = jnp.max(jnp.where(iota_g == i1, NEG_INF, blk), axis=0, keepdims=True)
        grp = jnp.where(iota_n == g, m1 + m2, grp)
    tick()
    gsel = jnp.zeros((N_GROUPS, tm), jnp.bool_)
    cur = grp
    for _ in range(TOPK_GROUPS):
        _, gi = _first_argmax(cur, iota_n, N_GROUPS)
        oh = iota_n == gi
        gsel = gsel | oh
        cur = jnp.where(oh, NEG_INF, cur)
    tick()
    gself = gsel.astype(F32)
    emask = jnp.concatenate(
        [jnp.broadcast_to(gself[g:g + 1, :], (gs, tm)) for g in range(N_GROUPS)], axis=0) > 0.5
    cur = jnp.where(emask, biased, NEG_INF)
    iota_e = lax.broadcasted_iota(I32, (E, tm), 0)
    sel = jnp.zeros((E, tm), jnp.bool_)
    picks = []
    for _ in range(TOP_K):
        _, ei = _first_argmax(cur, iota_e, E)
        oh = iota_e == ei
        sel = sel | oh
        cur = jnp.where(oh, NEG_INF, cur)
        picks.append(ei)
        tick()
    w = jnp.where(sel, scores, 0.0)
    w = w / jnp.sum(w, axis=0, keepdims=True) * ROUTED_SCALE
    return sel, w, picks, iota_e


def _post_kernel(ysw_ref, yfx_ref, gsw_ref, gfx_ref, x_ref, wsw_ref, wfx_ref, wo_ref, g2_ref,
                 wrt_ref, rb_ref, h1_ref, vp_ref, eidx_ref, rank_ref, wgt_ref, cnt_ref,
                 carry_ref, slab_scr, vprev_ref):
    i = pl.program_id(0)

    @pl.when(i == 0)
    def _():
        carry_ref[...] = jnp.zeros_like(carry_ref)
        vprev_ref[...] = jnp.zeros_like(vprev_ref)

    tm, D = x_ref.shape
    n_col = D // POST_COL_CHUNK

    def mixer():
        ysw, yfx = ysw_ref[...], yfx_ref[...]
        parts = []
        for c in range(n_col):
            cols = slice(c * POST_COL_CHUNK, (c + 1) * POST_COL_CHUNK)
            a = jnp.dot(ysw, wsw_ref[:, cols], preferred_element_type=F32)
            yield
            b = jnp.dot(yfx, wfx_ref[:, cols], preferred_element_type=F32)
            parts.append((gsw_ref[:, cols].astype(F32) * a + gfx_ref[:, cols].astype(F32) * b).astype(BF16))
            yield
        mix = jnp.concatenate(parts, axis=1)
        for c in range(n_col):
            cols = slice(c * POST_COL_CHUNK, (c + 1) * POST_COL_CHUNK)
            h1_ref[:, cols] = x_ref[:, cols] + jnp.dot(mix, wo_ref[:, cols], preferred_element_type=F32)
            yield

    chunks = mixer()
    tick = lambda: next(chunks, None)

    logits_t = lax.dot_general(wrt_ref[...], vprev_ref[...], (((1,), (1,)), ((), ())),
                               precision=HI, preferred_element_type=F32)
    tick()
    sel, w, picks, iota_e = _route(logits_t, rb_ref[...], tick)
    live = (i > 0).astype(F32)

    before = (lax.broadcasted_iota(I32, (tm, tm), 0) < lax.broadcasted_iota(I32, (tm, tm), 1)).astype(BF16)
    rank = jnp.dot(sel.astype(BF16), before, preferred_element_type=F32) + carry_ref[...]
    carry_ref[...] = carry_ref[...] + live * jnp.sum(sel.astype(F32), axis=1, keepdims=True)
    cnt_ref[...] = carry_ref[...]
    tick()

    iota_k = lax.broadcasted_iota(I32, (TOP_K, tm), 0)
    eidx = jnp.zeros((TOP_K, tm), I32)
    rnk = jnp.zeros((TOP_K, tm), F32)
    wgt = jnp.zeros((TOP_K, tm), F32)
    for k, ei in enumerate(picks):
        oh = iota_e == ei
        eidx = jnp.where(iota_k == k, ei, eidx)
        rnk = jnp.where(iota_k == k, jnp.sum(jnp.where(oh, rank, 0.0), axis=0, keepdims=True), rnk)
        wgt = jnp.where(iota_k == k, jnp.sum(jnp.where(oh, w, 0.0), axis=0, keepdims=True), wgt)
        tick()
    eidx_ref[...] = eidx
    rank_ref[...] = rnk.astype(I32)
    wgt_ref[...] = wgt

    for _ in chunks:
        pass
    v = _rms(h1_ref[...], g2_ref[...])
    _store_slabs(vp_ref, slab_scr, v)
    vprev_ref[...] = v


def _post(ysw, yfx, z, x2, wsw, wfx, wo, g2, wrt, rb, tm):
    T, D = x2.shape
    E = wrt.shape[0]
    nch = D // LANES
    n = T // tm
    const = lambda i: (0, 0)
    cur = lambda i: jnp.minimum(i, n - 1)
    prev = lambda i: jnp.maximum(i - 1, 0)
    return pl.pallas_call(
        _post_kernel,
        grid=(n + 1,),
        in_specs=[
            pl.BlockSpec((tm, SWA_Q_W), lambda i: (cur(i), 0)),
            pl.BlockSpec((tm, FOX_W), lambda i: (cur(i), 0)),
            pl.BlockSpec((tm, D), lambda i: (cur(i), 0)),
            pl.BlockSpec((tm, D), lambda i: (cur(i), 1)),
            pl.BlockSpec((tm, D), lambda i: (cur(i), 0)),
            pl.BlockSpec((SWA_Q_W, D), const),
            pl.BlockSpec((FOX_W, D), const),
            pl.BlockSpec((D, D), const),
            pl.BlockSpec((1, D), const),
            pl.BlockSpec((E, D), const),
            pl.BlockSpec((E, 1), const),
        ],
        out_specs=[
            pl.BlockSpec((tm, D), lambda i: (cur(i), 0)),
            pl.BlockSpec((tm * nch, LANES), lambda i: (cur(i), 0)),
            pl.BlockSpec((TOP_K, tm), lambda i: (0, prev(i))),
            pl.BlockSpec((TOP_K, tm), lambda i: (0, prev(i))),
            pl.BlockSpec((TOP_K, tm), lambda i: (0, prev(i))),
            pl.BlockSpec((E, 1), const),
        ],
        out_shape=[
            jax.ShapeDtypeStruct((T, D), F32),
            jax.ShapeDtypeStruct((T * nch, LANES), BF16),
            jax.ShapeDtypeStruct((TOP_K, T), I32),
            jax.ShapeDtypeStruct((TOP_K, T), I32),
            jax.ShapeDtypeStruct((TOP_K, T), F32),
            jax.ShapeDtypeStruct((E, 1), F32),
        ],
        scratch_shapes=[pltpu.VMEM((E, 1), F32), pltpu.VMEM((2, tm * nch // 2, LANES), F32),
                        pltpu.VMEM((tm, D), F32)],
        compiler_params=_cparams(1),
        name="post_mixer",
    )(ysw, yfx, z, z, x2, wsw, wfx, wo, g2, wrt, rb)


def _dispatch_kernel(zblk_ref, nz_ref, dest_ref, vp_ref, xs_out, zeros, sem, zsem, *, td, nch, blk_rows):
    @pl.when(pl.program_id(0) == 0)
    def _():
        zeros[...] = jnp.zeros_like(zeros)

        def zero_copy(blk):
            dst = pl.multiple_of(blk * blk_rows * nch, blk_rows * nch)
            return pltpu.make_async_copy(zeros, xs_out.at[pl.ds(dst, blk_rows * nch), :], zsem)

        def zstart(i, c):
            zero_copy(zblk_ref[i]).start()
            return c

        def zwait(i, c):
            zero_copy(0).wait()
            return c

        lax.fori_loop(0, nz_ref[0], zstart, 0)
        lax.fori_loop(0, nz_ref[0], zwait, 0)

    def row_copy(src_row, dst_row):
        return pltpu.make_async_copy(vp_ref.at[pl.ds(src_row, nch), :],
                                     xs_out.at[pl.ds(dst_row, nch), :], sem)

    def issue(t, c):
        src = pl.multiple_of(t * nch, nch)
        for k in range(TOP_K):
            row_copy(src, pl.multiple_of(dest_ref[t * TOP_K + k] * nch, nch)).start(priority=k % 2)
        return c

    lax.fori_loop(0, td, issue, 0)

    def drain(t, c):
        for k in range(TOP_K):
            row_copy(0, 0).wait()
        return c

    lax.fori_loop(0, td, drain, 0)


def _dispatch(zero_blocks, n_zero, dest_flat, vp, n_rows, td, blk_rows):
    nch = vp.shape[0] * TOP_K // dest_flat.shape[0]
    T = vp.shape[0] // nch
    return pl.pallas_call(
        functools.partial(_dispatch_kernel, td=td, nch=nch, blk_rows=blk_rows),
        grid_spec=pltpu.PrefetchScalarGridSpec(
            num_scalar_prefetch=2,
            grid=(T // td,),
            in_specs=[
                pl.BlockSpec((td * TOP_K,), lambda i, zb, nz: (i,), memory_space=pltpu.SMEM),
                pl.BlockSpec((td * nch, LANES), lambda i, zb, nz: (i, 0)),
            ],
            out_specs=pl.BlockSpec(memory_space=pl.ANY),
            scratch_shapes=[pltpu.VMEM((blk_rows * nch, LANES), BF16), pltpu.SemaphoreType.DMA(()),
                            pltpu.SemaphoreType.DMA(())],
        ),
        out_shape=jax.ShapeDtypeStruct((n_rows * nch, LANES), BF16),
        compiler_params=_cparams(1),
        name="dispatch",
    )(zero_blocks, n_zero, dest_flat, vp)


def _experts_kernel(bo_ref, el_ref, nu_ref, ne_ref, x_ref, wg_hbm, wu_hbm, wd_hbm, y_ref,
                    wg_f, wu_f, wd_f, wg_s, wu_s, wd_s, slab_scr, sems, *, rows, nch):
    b = pl.program_id(0)
    weights = ((wg_hbm, wg_f, wg_s), (wu_hbm, wu_f, wu_s), (wd_hbm, wd_f, wd_s))

    def fetch(o, slot):
        e = el_ref[o]
        return [pltpu.make_async_copy(w_hbm.at[e], w_f.at[slot], sems.at[slot, i])
                for i, (w_hbm, w_f, _) in enumerate(weights)]

    def start_fetch(o):
        @pl.when(o < ne_ref[0])
        def _():
            for copy in fetch(o, o % WEIGHT_SLOTS):
                copy.start(priority=1)

    @pl.when(b == 0)
    def _():
        for o in range(WEIGHT_SLOTS - 1):
            start_fetch(o)

    @pl.when(b < nu_ref[0])
    def _():
        o = bo_ref[b]
        slot = o % WEIGHT_SLOTS

        @pl.when((b == 0) | (o != bo_ref[jnp.maximum(b - 1, 0)]))
        def _():
            start_fetch(o + WEIGHT_SLOTS - 1)

            for copy, (_, w_f, w_s) in zip(fetch(o, slot), weights):
                copy.wait()
                w_s[...] = w_f[slot].astype(BF16)

        x = jnp.concatenate([c.astype(BF16) for c in _load_slabs(x_ref, 0, slab_scr, rows, nch)], axis=1)
        hg = jnp.dot(x, wg_s[...], preferred_element_type=F32)
        hu = jnp.dot(x, wu_s[...], preferred_element_type=F32)
        act = (jax.nn.silu(hg) * hu).astype(BF16)
        y = jnp.dot(act, wd_s[...], preferred_element_type=F32)
        _store_slabs(y_ref, slab_scr, y)

    @pl.when(b >= nu_ref[0])
    def _():
        y_ref[...] = jnp.zeros_like(y_ref)


def _experts(blk_ord, exp_list, n_used, n_exp_used, xs, wg, wu, wd, rows):
    E, D, FF = wg.shape
    nch = D // LANES
    nblk = xs.shape[0] // (rows * nch)
    xmap = lambda b, bo, el, nu, ne: (jnp.maximum(jnp.minimum(b, nu[0] - 1), 0), 0)
    return pl.pallas_call(
        functools.partial(_experts_kernel, rows=rows, nch=nch),
        grid_spec=pltpu.PrefetchScalarGridSpec(
            num_scalar_prefetch=4,
            grid=(nblk,),
            in_specs=[
                pl.BlockSpec((rows * nch, LANES), xmap),
                pl.BlockSpec(memory_space=pl.ANY),
                pl.BlockSpec(memory_space=pl.ANY),
                pl.BlockSpec(memory_space=pl.ANY),
            ],
            out_specs=pl.BlockSpec((rows * nch, LANES), lambda b, bo, el, nu, ne: (b, 0)),
            scratch_shapes=[pltpu.VMEM((WEIGHT_SLOTS, D, FF), F32), pltpu.VMEM((WEIGHT_SLOTS, D, FF), F32),
                            pltpu.VMEM((WEIGHT_SLOTS, FF, D), F32),
                            pltpu.VMEM((D, FF), BF16), pltpu.VMEM((D, FF), BF16), pltpu.VMEM((FF, D), BF16),
                            pltpu.VMEM((2, rows * nch // 2, LANES), F32),
                            pltpu.SemaphoreType.DMA((WEIGHT_SLOTS, 3))],
        ),
        out_shape=jax.ShapeDtypeStruct(xs.shape, BF16),
        compiler_params=_cparams(1),
        name="experts",
    )(blk_ord, exp_list, n_used, n_exp_used, xs, wg, wu, wd)


def _combine_kernel(dest_ref, dest_next_ref, wtok_ref, h1_ref, vp_ref, wsg_ref, wsu_ref, wsd_ref, gf_ref,
                    ys_hbm, out_ref, buf, slab_scr, sems, *, tc, nch):
    i = pl.program_id(0)
    slot = i % 2

    def row_copy(src_row, s, dst_row):
        return pltpu.make_async_copy(ys_hbm.at[pl.ds(src_row, nch), :],
                                     buf.at[s].at[pl.ds(dst_row, nch), :], sems.at[s])

    def gather(idx_ref, s):
        def issue(t, c):
            for k in range(TOP_K):
                row_copy(pl.multiple_of(idx_ref[t * TOP_K + k] * nch, nch), s,
                         pl.multiple_of((k * tc + t) * nch, nch)).start(priority=k % 2)
            return c

        lax.fori_loop(0, tc, issue, 0)

    @pl.when(i == 0)
    def _():
        gather(dest_ref, 0)

    x = jnp.concatenate([c.astype(BF16) for c in _load_slabs(vp_ref, 0, slab_scr, tc, nch)], axis=1)
    hg = jnp.dot(x, wsg_ref[...], preferred_element_type=F32)
    hu = jnp.dot(x, wsu_ref[...], preferred_element_type=F32)
    act = (jax.nn.silu(hg) * hu).astype(BF16)
    out_ref[...] = h1_ref[...] + jnp.dot(act, wsd_ref[...], preferred_element_type=F32)

    def drain(s):
        def body(t, c):
            for k in range(TOP_K):
                row_copy(0, s, 0).wait()
            return c

        lax.fori_loop(0, tc, body, 0)

    drain(slot)

    rows = buf.at[slot]
    per_chunk = tc // nch
    for k in range(TOP_K):
        wk = jnp.broadcast_to(wtok_ref[:, k:k + 1], (tc, LANES))
        _fill_slab_scratch(rows, k * tc * nch, slab_scr, tc, nch)
        for j in range(nch):
            for a in range((k * nch + j) * per_chunk, (k * nch + j + 1) * per_chunk):
                t, kk = divmod(a, TOP_K)
                row_copy(pl.multiple_of(dest_next_ref[a] * nch, nch), 1 - slot,
                         (kk * tc + t) * nch).start(priority=a % 2)
            out_ref[:, j * LANES:(j + 1) * LANES] += wk * _slab_chunk(slab_scr, j, tc, nch)
    out_ref[...] = _rms(out_ref[...], gf_ref[...])

    @pl.when(i == pl.num_programs(0) - 1)
    def _():
        drain(1 - slot)


def _combine(dest_flat, wtok, h1, vp, wsg, wsu, wsd, gf, ys, tc):
    T, D = h1.shape
    FF = wsg.shape[1]
    nch = D // LANES
    const = lambda i: (0, 0)
    n_steps = T // tc
    return pl.pallas_call(
        functools.partial(_combine_kernel, tc=tc, nch=nch),
        grid=(n_steps,),
        in_specs=[
            pl.BlockSpec((tc * TOP_K,), lambda i: (i,), memory_space=pltpu.SMEM),
            pl.BlockSpec((tc * TOP_K,), lambda i: (jnp.minimum(i + 1, n_steps - 1),), memory_space=pltpu.SMEM),
            pl.BlockSpec((tc, TOP_K), lambda i: (i, 0)),
            pl.BlockSpec((tc, D), lambda i: (i, 0)),
            pl.BlockSpec((tc * nch, LANES), lambda i: (i, 0)),
            pl.BlockSpec((D, FF), const),
            pl.BlockSpec((D, FF), const),
            pl.BlockSpec((FF, D), const),
            pl.BlockSpec((1, D), const),
            pl.BlockSpec(memory_space=pl.ANY),
        ],
        out_specs=pl.BlockSpec((tc, D), lambda i: (i, 0)),
        out_shape=jax.ShapeDtypeStruct((T, D), F32),
        scratch_shapes=[pltpu.VMEM((2, TOP_K * tc * nch, LANES), BF16),
                        pltpu.VMEM((2, tc * nch // 2, LANES), F32), pltpu.SemaphoreType.DMA((2,))],
        compiler_params=_cparams(1),
        name="combine",
    )(dest_flat, dest_flat, wtok, h1, vp, wsg, wsu, wsd, gf, ys)


def _layer(h, attn_norm_g, w_in, forget_bias, sinks, w_branch_swa, w_branch_fox, w_out, ffn_norm_g,
           w_router, router_bias, w_exp_gate, w_exp_up, w_exp_down, w_sh_gate, w_sh_up, w_sh_down, out_g):
    B, S, D = h.shape
    T = B * S
    E = w_router.shape[1]
    x2 = h.reshape(T, D)

    o = np.cumsum([0, SWA_Q_W, SWA_KV_W, SWA_KV_W, FOX_W, FOX_W, FOX_W, FORGET_W, D, D])
    aq, ak, av, fq, fk, fv, fg, gsw, gfx = [w_in[:, o[i]:o[i + 1]] for i in range(9)]

    def dup(w):
        w4 = w.reshape(D, SWA_KV_HEADS, 1, HEAD_DIM)
        return jnp.broadcast_to(w4, (D, SWA_KV_HEADS, 2, HEAD_DIM)).reshape(D, SWA_KV_EXP_W)

    w_main = jnp.concatenate([gsw, gfx, aq, dup(ak), dup(av), fq, fk, fv], axis=1).astype(BF16)
    spread = lambda a: jnp.pad(a[..., None], [(0, 0)] * a.ndim + [(0, FORGET_STRIDE - 1)]).reshape(*a.shape[:-1], LANES)
    w_f = spread(fg).astype(BF16)
    fb = spread(forget_bias.astype(F32).reshape(1, FORGET_W))

    z, f = _inproj(x2, attn_norm_g.reshape(1, D), w_main, w_f, tm=min(1024, T))
    ccol = _forget_cumsum(f, fb, B, S)
    y_swa = _swa(z, sinks.astype(F32), B, S, col0=2 * D)
    y_fox = _fox(z, ccol, B, S, col0=2 * D + SWA_Q_W + 2 * SWA_KV_EXP_W)

    h1, vp, eidx, rank, wgt, cnt = _post(
        y_swa, y_fox, z, x2, w_branch_swa.astype(BF16), w_branch_fox.astype(BF16), w_out.astype(BF16),
        ffn_norm_g.reshape(1, D), w_router.T, router_bias.reshape(E, 1).astype(F32), tm=min(256, T))

    rows = MOE_ROWS
    n_blocks = T * TOP_K // rows + E
    counts = cnt[:, 0].astype(I32)
    padded = (counts + rows - 1) // rows * rows
    pad_end = jnp.cumsum(padded)
    pad_start = pad_end - padded
    experts = jnp.arange(E, dtype=I32)[:, None, None]
    start_of = jnp.sum(jnp.where(eidx[None] == experts, pad_start[:, None, None], 0), axis=0)
    dest_flat = (start_of + rank).T.reshape(T * TOP_K)
    n_used = (pad_end[-1] // rows).astype(I32)
    blk = jnp.arange(n_blocks, dtype=I32)
    blk_exp = jnp.minimum(jnp.sum((pad_end[None, :] <= blk[:, None] * rows).astype(I32), axis=1), E - 1)
    e_ids = jnp.arange(E, dtype=I32)
    owns = counts > 0
    ord_of = jnp.cumsum(owns.astype(I32)) - 1
    n_exp_used = jnp.sum(owns.astype(I32))
    exp_list = jnp.sum(jnp.where((ord_of[None, :] == e_ids[:, None]) & owns[None, :], e_ids[None, :], 0), axis=1)
    blk_ord = jnp.sum(jnp.where(blk_exp[:, None] == e_ids[None, :], ord_of[None, :], 0), axis=1)

    last_of_ord = jnp.sum(jnp.where((ord_of[None, :] == e_ids[:, None]) & owns[None, :],
                                    (pad_end // rows - 1)[None, :], 0), axis=1)
    zi = jnp.arange(2 * E, dtype=I32)
    zero_blocks = jnp.where(zi < n_exp_used, jnp.concatenate([last_of_ord, last_of_ord]),
                            n_used + zi - n_exp_used)
    n_zero = n_exp_used + n_blocks - n_used

    xs = _dispatch(zero_blocks, n_zero.reshape(1), dest_flat, vp, n_blocks * rows, td=min(256, T), blk_rows=rows)
    ys = _experts(blk_ord, exp_list, n_used.reshape(1), n_exp_used.reshape(1), xs,
                  w_exp_gate, w_exp_up, w_exp_down, rows)
    out = _combine(dest_flat, wgt.T, h1, vp, w_sh_gate.astype(BF16), w_sh_up.astype(BF16),
                   w_sh_down.astype(BF16), out_g.reshape(1, D), ys, tc=min(256, T))
    return out.reshape(B, S, D)


def kernel(x, attn_norm_g, w_in, forget_bias, sinks, w_branch_swa, w_branch_fox, w_out, ffn_norm_g,
           w_router, router_bias, w_exp_gate, w_exp_up, w_exp_down, w_sh_gate, w_sh_up, w_sh_down,
           final_norm_g):
    depth = w_in.shape[0]
    assert depth == 1, "the final norm is fused into the last layer's combine"
    return _layer(x, attn_norm_g[0], w_in[0], forget_bias[0], sinks[0], w_branch_swa[0], w_branch_fox[0],
                  w_out[0], ffn_norm_g[0], w_router[0], router_bias[0], w_exp_gate[0], w_exp_up[0],
                  w_exp_down[0], w_sh_gate[0], w_sh_up[0], w_sh_down[0], final_norm_g)
```

```python
import functools

import numpy as np
import jax
import jax.numpy as jnp
from jax import lax
from jax.experimental import pallas as pl
from jax.experimental.pallas import tpu as pltpu

F32 = jnp.float32
BF16 = jnp.bfloat16
I32 = jnp.int32

HEAD_DIM = 64
SWA_Q_HEADS = 16
SWA_KV_HEADS = 4
FOX_HEADS = 16
WINDOW = 128
BLOCK = 128
N_GROUPS = 8
TOPK_GROUPS = 4
TOP_K = 8
ROUTED_SCALE = 2.5
RMS_EPS = 1e-6
LANES = 128
SWA_Q_W = SWA_Q_HEADS * HEAD_DIM
SWA_KV_W = SWA_KV_HEADS * HEAD_DIM
FOX_W = FOX_HEADS * HEAD_DIM
FORGET_W = FOX_HEADS
FORGET_STRIDE = 8
SWA_KV_EXP_W = 2 * SWA_KV_W
Z_TILE = 1024
FOX_TQ = 128
FOX_TK = 256
FOX_SCORE_LEAD = 2
FOX_PAIRS_PER_LOOP = 8
MOE_ROWS = 256
POST_COL_CHUNK = 512
WEIGHT_SLOTS = 3
VMEM_LIMIT = 56 * 1024 * 1024
HI = lax.Precision.HIGHEST
NEG_INF = float("-inf")


def _cparams(n_axes):
    return pltpu.CompilerParams(dimension_semantics=("arbitrary",) * n_axes,
                                vmem_limit_bytes=VMEM_LIMIT)


def _rms(x, g):
    return x * lax.rsqrt(jnp.mean(x * x, axis=-1, keepdims=True) + RMS_EPS) * g


def _store_slabs(ref, scr, x):
    rows = x.shape[0]
    n = x.shape[1] // LANES
    hn = n // 2
    for j in range(n):
        scr[j // hn, pl.ds(j % hn, rows, stride=hn), :] = x[:, j * LANES:(j + 1) * LANES]
    halves = [scr[i].reshape(rows, 1, hn, LANES) for i in range(2)]
    ref[...] = jnp.concatenate(halves, axis=1).reshape(rows * n, LANES).astype(BF16)


def _fill_slab_scratch(ref, base, scr, rows, n):
    hn = n // 2
    xf = ref[pl.ds(base, rows * n), :].astype(F32).reshape(rows, 2, hn, LANES)
    for i in range(2):
        scr[i] = xf[:, i].reshape(rows * hn, LANES)


def _slab_chunk(scr, j, rows, n):
    hn = n // 2
    return scr[j // hn, pl.ds(j % hn, rows, stride=hn), :]


def _load_slabs(ref, base, scr, rows, n):
    _fill_slab_scratch(ref, base, scr, rows, n)
    return [_slab_chunk(scr, j, rows, n) for j in range(n)]


def _inproj_kernel(x_ref, g_ref, w_ref, wf_ref, z_ref, f_ref, u_scr, *, n_gate_tiles):
    j = pl.program_id(1)

    @pl.when(j == 0)
    def _():
        u = _rms(x_ref[...], g_ref[...]).astype(BF16)
        u_scr[...] = u
        f_ref[...] = jnp.dot(u, wf_ref[...], preferred_element_type=F32)

    z = jnp.dot(u_scr[...], w_ref[...], preferred_element_type=F32)

    @pl.when(j < n_gate_tiles)
    def _():
        z_ref[...] = jax.nn.sigmoid(z).astype(BF16)

    @pl.when(j >= n_gate_tiles)
    def _():
        z_ref[...] = z.astype(BF16)


def _inproj(x2, g, w_main, w_f, tm):
    T, D = x2.shape
    ZW = w_main.shape[1]
    return pl.pallas_call(
        functools.partial(_inproj_kernel, n_gate_tiles=2 * D // Z_TILE),
        grid=(T // tm, ZW // Z_TILE),
        in_specs=[
            pl.BlockSpec((tm, D), lambda i, j: (i, 0)),
            pl.BlockSpec((1, D), lambda i, j: (0, 0)),
            pl.BlockSpec((D, Z_TILE), lambda i, j: (0, j)),
            pl.BlockSpec((D, LANES), lambda i, j: (0, 0)),
        ],
        out_specs=[
            pl.BlockSpec((tm, Z_TILE), lambda i, j: (i, j)),
            pl.BlockSpec((tm, LANES), lambda i, j: (i, 0)),
        ],
        out_shape=[jax.ShapeDtypeStruct((T, ZW), BF16), jax.ShapeDtypeStruct((T, LANES), F32)],
        scratch_shapes=[pltpu.VMEM((tm, D), BF16)],
        compiler_params=_cparams(2),
        name="inproj",
    )(x2, g, w_main, w_f)


def _fcum_kernel(f_ref, b_ref, ccol_ref, carry_ref):
    @pl.when(pl.program_id(1) == 0)
    def _():
        carry_ref[...] = jnp.zeros_like(carry_ref)

    a = f_ref[...] + b_ref[...]
    ls = jnp.minimum(a, 0.0) - jnp.log1p(jnp.exp(-jnp.abs(a)))
    n = ls.shape[0]
    tri = (lax.broadcasted_iota(I32, (n, n), 1) <= lax.broadcasted_iota(I32, (n, n), 0)).astype(F32)
    c = jnp.dot(tri, ls, precision=HI, preferred_element_type=F32) + carry_ref[...]
    ccol_ref[...] = c
    carry_ref[...] = c[n - 1:n, :]


def _forget_cumsum(f, bias, B, S):
    T = B * S
    nb = S // FOX_TK
    return pl.pallas_call(
        _fcum_kernel,
        grid=(B, nb),
        in_specs=[
            pl.BlockSpec((FOX_TK, LANES), lambda b, j: (b * nb + j, 0)),
            pl.BlockSpec((1, LANES), lambda b, j: (0, 0)),
        ],
        out_specs=pl.BlockSpec((FOX_TK, LANES), lambda b, j: (b * nb + j, 0)),
        out_shape=jax.ShapeDtypeStruct((T, LANES), F32),
        scratch_shapes=[pltpu.VMEM((1, LANES), F32)],
        compiler_params=_cparams(2),
        name="forget_cumsum",
    )(f, bias)


def _swa_kernel(sink_ref, q_ref, kp_ref, kc_ref, vp_ref, vc_ref, y_ref, *, slopes):
    qi = pl.program_id(1)
    kk = jnp.concatenate([kp_ref[...], kc_ref[...]], axis=0)
    vv = jnp.concatenate([vp_ref[...], vc_ref[...]], axis=0)
    row = lax.broadcasted_iota(I32, (BLOCK, 2 * BLOCK), 0)
    col = lax.broadcasted_iota(I32, (BLOCK, 2 * BLOCK), 1)
    dist = row + BLOCK - col
    valid = (dist >= 0) & (dist < WINDOW) & ((qi > 0) | (col >= BLOCK))
    distf = dist.astype(F32)
    lo_half = lax.broadcasted_iota(I32, (BLOCK, LANES), 1) < HEAD_DIM
    group = SWA_Q_HEADS // SWA_KV_HEADS
    for pair in range(SWA_Q_HEADS // 2):
        q2 = q_ref[:, pair * LANES:(pair + 1) * LANES]
        g = (2 * pair) // group
        kg = kk[:, g * LANES:(g + 1) * LANES]
        vg = vv[:, g * LANES:(g + 1) * LANES]
        outs = []
        for half in range(2):
            h = 2 * pair + half
            keep = lo_half if half == 0 else jnp.logical_not(lo_half)
            qm = jnp.where(keep, q2, jnp.zeros_like(q2)) * jnp.asarray(HEAD_DIM ** -0.5, BF16)
            s = lax.dot_general(qm, kg, (((1,), (1,)), ((), ())), preferred_element_type=F32)
            s = jnp.where(valid, s - slopes[h] * distf, NEG_INF)
            sink = sink_ref[h]
            m = jnp.maximum(jnp.max(s, axis=-1, keepdims=True), sink)
            p = jnp.exp(s - m)
            denom = jnp.sum(p, axis=-1, keepdims=True) + jnp.exp(sink - m)
            o = jnp.dot(p.astype(BF16), vg, preferred_element_type=F32)
            outs.append(o / denom)
        y_ref[:, pair * LANES:(pair + 1) * LANES] = jnp.where(lo_half, outs[0], outs[1]).astype(BF16)


def _swa(z, sinks, B, S, col0):
    T = B * S
    nq = S // BLOCK
    qb = col0 // SWA_Q_W
    kb = (col0 + SWA_Q_W) // SWA_KV_EXP_W
    vb = kb + 1
    slopes = tuple(2.0 ** (-8.0 * (h + 1) / SWA_Q_HEADS) for h in range(SWA_Q_HEADS))
    cur = lambda b, i: b * nq + i
    prev = lambda b, i: jnp.maximum(b * nq + i - 1, 0)
    return pl.pallas_call(
        functools.partial(_swa_kernel, slopes=slopes),
        grid=(B, nq),
        in_specs=[
            pl.BlockSpec(memory_space=pltpu.SMEM),
            pl.BlockSpec((BLOCK, SWA_Q_W), lambda b, i: (cur(b, i), qb)),
            pl.BlockSpec((BLOCK, SWA_KV_EXP_W), lambda b, i: (prev(b, i), kb)),
            pl.BlockSpec((BLOCK, SWA_KV_EXP_W), lambda b, i: (cur(b, i), kb)),
            pl.BlockSpec((BLOCK, SWA_KV_EXP_W), lambda b, i: (prev(b, i), vb)),
            pl.BlockSpec((BLOCK, SWA_KV_EXP_W), lambda b, i: (cur(b, i), vb)),
        ],
        out_specs=pl.BlockSpec((BLOCK, SWA_Q_W), lambda b, i: (cur(b, i), 0)),
        out_shape=jax.ShapeDtypeStruct((T, SWA_Q_W), BF16),
        compiler_params=_cparams(2),
        name="swa",
    )(sinks, z, z, z, z, z)


def _split3(x):
    hi = x.astype(BF16).astype(F32)
    r1 = x - hi
    mid = r1.astype(BF16).astype(F32)
    lo = (r1 - mid).astype(BF16).astype(F32)
    return hi, mid, lo


def _split3_adjacent(c, lane):
    hi, mid, lo = _split3(jnp.where((lane & (FORGET_STRIDE - 1)) == 0, c, 0.0))
    return hi + pltpu.roll(mid, 1, axis=1) + pltpu.roll(lo, 2, axis=1)


def _aug_lanes(lane, terms, h, val0, one0):
    ones = jnp.where((lane >= one0) & (lane < one0 + 3), 1.0, 0.0)
    moved = pltpu.roll(terms, (val0 - h * FORGET_STRIDE) % LANES, axis=1)
    return jnp.where((lane >= val0) & (lane < val0 + 3), moved, ones)


def _fox_kernel(q_ref, k_ref, v_ref, c_ref, y_ref, ka_ref, va_ref):
    qi = pl.program_id(1)
    S = k_ref.shape[0]
    lane_k = lax.broadcasted_iota(I32, (FOX_TK, LANES), 1)
    lane_q = lax.broadcasted_iota(I32, (FOX_TQ, LANES), 1)

    def head_lanes(h):
        return slice(h * LANES, (h + 1) * LANES)

    def spare0(h):
        return HEAD_DIM if h % 2 == 0 else 0

    def keep(lane, h):
        return (lane < HEAD_DIM) if h % 2 == 0 else (lane >= HEAD_DIM)

    @pl.when(qi == 0)
    def _():
        def build(r, carry):
            r0 = pl.multiple_of(r * FOX_TK, FOX_TK)
            parts = _split3_adjacent(-c_ref[pl.ds(r0, FOX_TK), :], lane_k)
            for pair in range(FOX_HEADS // 2):
                kp = k_ref[pl.ds(r0, FOX_TK), pair * LANES:(pair + 1) * LANES].astype(F32)
                vp = v_ref[pl.ds(r0, FOX_TK), pair * LANES:(pair + 1) * LANES]
                for h in (2 * pair, 2 * pair + 1):
                    extra = _aug_lanes(lane_k, parts, h, spare0(h), spare0(h) + 3)
                    ka_ref[pl.ds(r0, FOX_TK), head_lanes(h)] = jnp.where(keep(lane_k, h), kp, extra).astype(BF16)
                    va_ref[pl.ds(r0, FOX_TK), head_lanes(h)] = jnp.where(keep(lane_k, h), vp, jnp.ones_like(vp))
            return carry

        lax.fori_loop(0, S // FOX_TK, build, 0)

    n_full = qi * FOX_TQ // FOX_TK
    row = qi * FOX_TQ + lax.broadcasted_iota(I32, (FOX_TQ, FOX_TK), 0)
    col = lax.broadcasted_iota(I32, (FOX_TQ, FOX_TK), 1)
    ct_parts = _split3_adjacent(c_ref[pl.ds(pl.multiple_of(qi * FOX_TQ, FOX_TQ), FOX_TQ), :], lane_q)

    hpl = 2 * FOX_PAIRS_PER_LOOP
    for h0 in range(0, FOX_HEADS, hpl):
        heads = list(range(h0, h0 + hpl))
        qas = []
        for h in heads:
            q2 = q_ref[:, (h // 2) * LANES:(h // 2 + 1) * LANES].astype(F32) * (HEAD_DIM ** -0.5)
            extra = _aug_lanes(lane_q, ct_parts, h, spare0(h) + 3, spare0(h))
            qas.append(jnp.where(keep(lane_q, h), q2, extra).astype(BF16))

        def step(j, carry, masked, heads=heads, qas=qas):
            k0 = pl.multiple_of(j * FOX_TK, FOX_TK)

            def scores(i):
                return lax.dot_general(qas[i], ka_ref[pl.ds(k0, FOX_TK), head_lanes(heads[i])],
                                       (((1,), (1,)), ((), ())), preferred_element_type=F32)

            ready = [scores(i) for i in range(min(FOX_SCORE_LEAD, len(heads)))]
            out = []
            for i, h in enumerate(heads):
                m, acc = carry[2 * i:2 * i + 2]
                if i + FOX_SCORE_LEAD < len(heads):
                    ready.append(scores(i + FOX_SCORE_LEAD))
                s = ready.pop(0)
                if masked:
                    s = jnp.where(col + j * FOX_TK <= row, s, NEG_INF)
                m_new = jnp.maximum(m, jnp.max(s, axis=-1, keepdims=True))
                p = jnp.exp(s - m_new).astype(BF16)
                acc = jnp.exp(m - m_new) * acc + jnp.dot(p, va_ref[pl.ds(k0, FOX_TK), head_lanes(h)],
                                                         preferred_element_type=F32)
                out += [m_new, acc]
            return tuple(out)

        init = (jnp.full((FOX_TQ, 1), -1e30, F32), jnp.zeros((FOX_TQ, LANES), F32)) * hpl
        carry = lax.fori_loop(0, n_full, functools.partial(step, masked=False), init)
        carry = step(n_full, carry, True)
        for i in range(0, hpl, 2):
            acc_lo, acc_hi = carry[2 * i + 1], carry[2 * i + 3]
            o_lo = acc_lo / acc_lo[:, LANES - 1:LANES]
            o_hi = acc_hi / acc_hi[:, 0:1]
            pair = heads[i] // 2
            y_ref[:, pair * LANES:(pair + 1) * LANES] = jnp.where(lane_q < HEAD_DIM, o_lo, o_hi).astype(BF16)


def _fox(z, ccol, B, S, col0):
    T = B * S
    nq = S // FOX_TQ
    qb = col0 // FOX_W
    return pl.pallas_call(
        _fox_kernel,
        grid=(B, nq),
        in_specs=[
            pl.BlockSpec((FOX_TQ, FOX_W), lambda b, i: (b * nq + i, qb)),
            pl.BlockSpec((S, FOX_W), lambda b, i: (b, qb + 1)),
            pl.BlockSpec((S, FOX_W), lambda b, i: (b, qb + 2)),
            pl.BlockSpec((S, LANES), lambda b, i: (b, 0)),
        ],
        out_specs=pl.BlockSpec((FOX_TQ, FOX_W), lambda b, i: (b * nq + i, 0)),
        out_shape=jax.ShapeDtypeStruct((T, FOX_W), BF16),
        scratch_shapes=[pltpu.VMEM((S, FOX_HEADS * LANES), BF16), pltpu.VMEM((S, FOX_HEADS * LANES), BF16)],
        compiler_params=_cparams(2),
        name="fox",
    )(z, z, z, ccol)


def _first_argmax(vals, iota, n):
    m = jnp.max(vals, axis=0, keepdims=True)
    idx = jnp.min(jnp.where(vals == m, iota, n), axis=0, keepdims=True)
    return m, idx


def _route(logits_t, rb, tick=lambda: None):
    E, tm = logits_t.shape
    gs = E // N_GROUPS
    scores = jax.nn.sigmoid(logits_t)
    biased = scores + rb
    iota_g = lax.broadcasted_iota(I32, (gs, tm), 0)
    iota_n = lax.broadcasted_iota(I32, (N_GROUPS, tm), 0)
    grp = jnp.zeros((N_GROUPS, tm), F32)
    for g in range(N_GROUPS):
        blk = biased[g * gs:(g + 1) * gs, :]
        m1, i1 = _first_argmax(blk, iota_g, gs)
        m2 = jnp.max(jnp.where(iota_g == i1, NEG_INF, blk), axis=0, keepdims=True)
        grp = jnp.where(iota_n == g, m1 + m2, grp)
    tick()
    gsel = jnp.zeros((N_GROUPS, tm), jnp.bool_)
    cur = grp
    for _ in range(TOPK_GROUPS):
        _, gi = _first_argmax(cur, iota_n, N_GROUPS)
        oh = iota_n == gi
        gsel = gsel | oh
        cur = jnp.where(oh, NEG_INF, cur)
    tick()
    gself = gsel.astype(F32)
    emask = jnp.concatenate(
        [jnp.broadcast_to(gself[g:g + 1, :], (gs, tm)) for g in range(N_GROUPS)], axis=0) > 0.5
    cur = jnp.where(emask, biased, NEG_INF)
    iota_e = lax.broadcasted_iota(I32, (E, tm), 0)
    sel = jnp.zeros((E, tm), jnp.bool_)
    picks = []
    for _ in range(TOP_K):
        _, ei = _first_argmax(cur, iota_e, E)
        oh = iota_e == ei
        sel = sel | oh
        cur = jnp.where(oh, NEG_INF, cur)
        picks.append(ei)
        tick()
    w = jnp.where(sel, scores, 0.0)
    w = w / jnp.sum(w, axis=0, keepdims=True) * ROUTED_SCALE
    return sel, w, picks, iota_e


def _post_kernel(ysw_ref, yfx_ref, gsw_ref, gfx_ref, x_ref, wsw_ref, wfx_ref, wo_ref, g2_ref,
                 wrt_ref, rb_ref, h1_ref, vp_ref, eidx_ref, rank_ref, wgt_ref, cnt_ref,
                 carry_ref, slab_scr, vprev_ref):
    i = pl.program_id(0)

    @pl.when(i == 0)
    def _():
        carry_ref[...] = jnp.zeros_like(carry_ref)
        vprev_ref[...] = jnp.zeros_like(vprev_ref)

    tm, D = x_ref.shape
    n_col = D // POST_COL_CHUNK

    def mixer():
        ysw, yfx = ysw_ref[...], yfx_ref[...]
        parts = []
        for c in range(n_col):
            cols = slice(c * POST_COL_CHUNK, (c + 1) * POST_COL_CHUNK)
            a = jnp.dot(ysw, wsw_ref[:, cols], preferred_element_type=F32)
            yield
            b = jnp.dot(yfx, wfx_ref[:, cols], preferred_element_type=F32)
            parts.append((gsw_ref[:, cols].astype(F32) * a + gfx_ref[:, cols].astype(F32) * b).astype(BF16))
            yield
        mix = jnp.concatenate(parts, axis=1)
        for c in range(n_col):
            cols = slice(c * POST_COL_CHUNK, (c + 1) * POST_COL_CHUNK)
            h1_ref[:, cols] = x_ref[:, cols] + jnp.dot(mix, wo_ref[:, cols], preferred_element_type=F32)
            yield

    chunks = mixer()
    tick = lambda: next(chunks, None)

    logits_t = lax.dot_general(wrt_ref[...], vprev_ref[...], (((1,), (1,)), ((), ())),
                               precision=HI, preferred_element_type=F32)
    tick()
    sel, w, picks, iota_e = _route(logits_t, rb_ref[...], tick)
    live = (i > 0).astype(F32)

    before = (lax.broadcasted_iota(I32, (tm, tm), 0) < lax.broadcasted_iota(I32, (tm, tm), 1)).astype(BF16)
    rank = jnp.dot(sel.astype(BF16), before, preferred_element_type=F32) + carry_ref[...]
    carry_ref[...] = carry_ref[...] + live * jnp.sum(sel.astype(F32), axis=1, keepdims=True)
    cnt_ref[...] = carry_ref[...]
    tick()

    iota_k = lax.broadcasted_iota(I32, (TOP_K, tm), 0)
    eidx = jnp.zeros((TOP_K, tm), I32)
    rnk = jnp.zeros((TOP_K, tm), F32)
    wgt = jnp.zeros((TOP_K, tm), F32)
    for k, ei in enumerate(picks):
        oh = iota_e == ei
        eidx = jnp.where(iota_k == k, ei, eidx)
        rnk = jnp.where(iota_k == k, jnp.sum(jnp.where(oh, rank, 0.0), axis=0, keepdims=True), rnk)
        wgt = jnp.where(iota_k == k, jnp.sum(jnp.where(oh, w, 0.0), axis=0, keepdims=True), wgt)
        tick()
    eidx_ref[...] = eidx
    rank_ref[...] = rnk.astype(I32)
    wgt_ref[...] = wgt

    for _ in chunks:
        pass
    v = _rms(h1_ref[...], g2_ref[...])
    _store_slabs(vp_ref, slab_scr, v)
    vprev_ref[...] = v


def _post(ysw, yfx, z, x2, wsw, wfx, wo, g2, wrt, rb, tm):
    T, D = x2.shape
    E = wrt.shape[0]
    nch = D // LANES
    n = T // tm
    const = lambda i: (0, 0)
    cur = lambda i: jnp.minimum(i, n - 1)
    prev = lambda i: jnp.maximum(i - 1, 0)
    return pl.pallas_call(
        _post_kernel,
        grid=(n + 1,),
        in_specs=[
            pl.BlockSpec((tm, SWA_Q_W), lambda i: (cur(i), 0)),
            pl.BlockSpec((tm, FOX_W), lambda i: (cur(i), 0)),
            pl.BlockSpec((tm, D), lambda i: (cur(i), 0)),
            pl.BlockSpec((tm, D), lambda i: (cur(i), 1)),
            pl.BlockSpec((tm, D), lambda i: (cur(i), 0)),
            pl.BlockSpec((SWA_Q_W, D), const),
            pl.BlockSpec((FOX_W, D), const),
            pl.BlockSpec((D, D), const),
            pl.BlockSpec((1, D), const),
            pl.BlockSpec((E, D), const),
            pl.BlockSpec((E, 1), const),
        ],
        out_specs=[
            pl.BlockSpec((tm, D), lambda i: (cur(i), 0)),
            pl.BlockSpec((tm * nch, LANES), lambda i: (cur(i), 0)),
            pl.BlockSpec((TOP_K, tm), lambda i: (0, prev(i))),
            pl.BlockSpec((TOP_K, tm), lambda i: (0, prev(i))),
            pl.BlockSpec((TOP_K, tm), lambda i: (0, prev(i))),
            pl.BlockSpec((E, 1), const),
        ],
        out_shape=[
            jax.ShapeDtypeStruct((T, D), F32),
            jax.ShapeDtypeStruct((T * nch, LANES), BF16),
            jax.ShapeDtypeStruct((TOP_K, T), I32),
            jax.ShapeDtypeStruct((TOP_K, T), I32),
            jax.ShapeDtypeStruct((TOP_K, T), F32),
            jax.ShapeDtypeStruct((E, 1), F32),
        ],
        scratch_shapes=[pltpu.VMEM((E, 1), F32), pltpu.VMEM((2, tm * nch // 2, LANES), F32),
                        pltpu.VMEM((tm, D), F32)],
        compiler_params=_cparams(1),
        name="post_mixer",
    )(ysw, yfx, z, z, x2, wsw, wfx, wo, g2, wrt, rb)


def _dispatch_kernel(zblk_ref, nz_ref, dest_ref, vp_ref, xs_out, zeros, sem, zsem, *, td, nch, blk_rows):
    @pl.when(pl.program_id(0) == 0)
    def _():
        zeros[...] = jnp.zeros_like(zeros)

        def zero_copy(blk):
            dst = pl.multiple_of(blk * blk_rows * nch, blk_rows * nch)
            return pltpu.make_async_copy(zeros, xs_out.at[pl.ds(dst, blk_rows * nch), :], zsem)

        def zstart(i, c):
            zero_copy(zblk_ref[i]).start()
            return c

        def zwait(i, c):
            zero_copy(0).wait()
            return c

        lax.fori_loop(0, nz_ref[0], zstart, 0)
        lax.fori_loop(0, nz_ref[0], zwait, 0)

    def row_copy(src_row, dst_row):
        return pltpu.make_async_copy(vp_ref.at[pl.ds(src_row, nch), :],
                                     xs_out.at[pl.ds(dst_row, nch), :], sem)

    def issue(t, c):
        src = pl.multiple_of(t * nch, nch)
        for k in range(TOP_K):
            row_copy(src, pl.multiple_of(dest_ref[t * TOP_K + k] * nch, nch)).start(priority=k % 2)
        return c

    lax.fori_loop(0, td, issue, 0)

    def drain(t, c):
        for k in range(TOP_K):
            row_copy(0, 0).wait()
        return c

    lax.fori_loop(0, td, drain, 0)


def _dispatch(zero_blocks, n_zero, dest_flat, vp, n_rows, td, blk_rows):
    nch = vp.shape[0] * TOP_K // dest_flat.shape[0]
    T = vp.shape[0] // nch
    return pl.pallas_call(
        functools.partial(_dispatch_kernel, td=td, nch=nch, blk_rows=blk_rows),
        grid_spec=pltpu.PrefetchScalarGridSpec(
            num_scalar_prefetch=2,
            grid=(T // td,),
            in_specs=[
                pl.BlockSpec((td * TOP_K,), lambda i, zb, nz: (i,), memory_space=pltpu.SMEM),
                pl.BlockSpec((td * nch, LANES), lambda i, zb, nz: (i, 0)),
            ],
            out_specs=pl.BlockSpec(memory_space=pl.ANY),
            scratch_shapes=[pltpu.VMEM((blk_rows * nch, LANES), BF16), pltpu.SemaphoreType.DMA(()),
                            pltpu.SemaphoreType.DMA(())],
        ),
        out_shape=jax.ShapeDtypeStruct((n_rows * nch, LANES), BF16),
        compiler_params=_cparams(1),
        name="dispatch",
    )(zero_blocks, n_zero, dest_flat, vp)


def _experts_kernel(bo_ref, el_ref, nu_ref, ne_ref, x_ref, wg_hbm, wu_hbm, wd_hbm, y_ref,
                    wg_f, wu_f, wd_f, wg_s, wu_s, wd_s, slab_scr, sems, *, rows, nch):
    b = pl.program_id(0)
    weights = ((wg_hbm, wg_f, wg_s), (wu_hbm, wu_f, wu_s), (wd_hbm, wd_f, wd_s))

    def fetch(o, slot):
        e = el_ref[o]
        return [pltpu.make_async_copy(w_hbm.at[e], w_f.at[slot], sems.at[slot, i])
                for i, (w_hbm, w_f, _) in enumerate(weights)]

    def start_fetch(o):
        @pl.when(o < ne_ref[0])
        def _():
            for copy in fetch(o, o % WEIGHT_SLOTS):
                copy.start(priority=1)

    @pl.when(b == 0)
    def _():
        for o in range(WEIGHT_SLOTS - 1):
            start_fetch(o)

    @pl.when(b < nu_ref[0])
    def _():
        o = bo_ref[b]
        slot = o % WEIGHT_SLOTS

        @pl.when((b == 0) | (o != bo_ref[jnp.maximum(b - 1, 0)]))
        def _():
            start_fetch(o + WEIGHT_SLOTS - 1)

            for copy, (_, w_f, w_s) in zip(fetch(o, slot), weights):
                copy.wait()
                w_s[...] = w_f[slot].astype(BF16)

        x = jnp.concatenate([c.astype(BF16) for c in _load_slabs(x_ref, 0, slab_scr, rows, nch)], axis=1)
        hg = jnp.dot(x, wg_s[...], preferred_element_type=F32)
        hu = jnp.dot(x, wu_s[...], preferred_element_type=F32)
        act = (jax.nn.silu(hg) * hu).astype(BF16)
        y = jnp.dot(act, wd_s[...], preferred_element_type=F32)
        _store_slabs(y_ref, slab_scr, y)

    @pl.when(b >= nu_ref[0])
    def _():
        y_ref[...] = jnp.zeros_like(y_ref)


def _experts(blk_ord, exp_list, n_used, n_exp_used, xs, wg, wu, wd, rows):
    E, D, FF = wg.shape
    nch = D // LANES
    nblk = xs.shape[0] // (rows * nch)
    xmap = lambda b, bo, el, nu, ne: (jnp.maximum(jnp.minimum(b, nu[0] - 1), 0), 0)
    return pl.pallas_call(
        functools.partial(_experts_kernel, rows=rows, nch=nch),
        grid_spec=pltpu.PrefetchScalarGridSpec(
            num_scalar_prefetch=4,
            grid=(nblk,),
            in_specs=[
                pl.BlockSpec((rows * nch, LANES), xmap),
                pl.BlockSpec(memory_space=pl.ANY),
                pl.BlockSpec(memory_space=pl.ANY),
                pl.BlockSpec(memory_space=pl.ANY),
            ],
            out_specs=pl.BlockSpec((rows * nch, LANES), lambda b, bo, el, nu, ne: (b, 0)),
            scratch_shapes=[pltpu.VMEM((WEIGHT_SLOTS, D, FF), F32), pltpu.VMEM((WEIGHT_SLOTS, D, FF), F32),
                            pltpu.VMEM((WEIGHT_SLOTS, FF, D), F32),
                            pltpu.VMEM((D, FF), BF16), pltpu.VMEM((D, FF), BF16), pltpu.VMEM((FF, D), BF16),
                            pltpu.VMEM((2, rows * nch // 2, LANES), F32),
                            pltpu.SemaphoreType.DMA((WEIGHT_SLOTS, 3))],
        ),
        out_shape=jax.ShapeDtypeStruct(xs.shape, BF16),
        compiler_params=_cparams(1),
        name="experts",
    )(blk_ord, exp_list, n_used, n_exp_used, xs, wg, wu, wd)


def _combine_kernel(dest_ref, dest_next_ref, wtok_ref, h1_ref, vp_ref, wsg_ref, wsu_ref, wsd_ref, gf_ref,
                    ys_hbm, out_ref, buf, slab_scr, sems, *, tc, nch):
    i = pl.program_id(0)
    slot = i % 2

    def row_copy(src_row, s, dst_row):
        return pltpu.make_async_copy(ys_hbm.at[pl.ds(src_row, nch), :],
                                     buf.at[s].at[pl.ds(dst_row, nch), :], sems.at[s])

    def gather(idx_ref, s):
        def issue(t, c):
            for k in range(TOP_K):
                row_copy(pl.multiple_of(idx_ref[t * TOP_K + k] * nch, nch), s,
                         pl.multiple_of((k * tc + t) * nch, nch)).start(priority=k % 2)
            return c

        lax.fori_loop(0, tc, issue, 0)

    @pl.when(i == 0)
    def _():
        gather(dest_ref, 0)

    x = jnp.concatenate([c.astype(BF16) for c in _load_slabs(vp_ref, 0, slab_scr, tc, nch)], axis=1)
    hg = jnp.dot(x, wsg_ref[...], preferred_element_type=F32)
    hu = jnp.dot(x, wsu_ref[...], preferred_element_type=F32)
    act = (jax.nn.silu(hg) * hu).astype(BF16)
    out_ref[...] = h1_ref[...] + jnp.dot(act, wsd_ref[...], preferred_element_type=F32)

    def drain(s):
        def body(t, c):
            for k in range(TOP_K):
                row_copy(0, s, 0).wait()
            return c

        lax.fori_loop(0, tc, body, 0)

    drain(slot)

    rows = buf.at[slot]
    per_chunk = tc // nch
    for k in range(TOP_K):
        wk = jnp.broadcast_to(wtok_ref[:, k:k + 1], (tc, LANES))
        _fill_slab_scratch(rows, k * tc * nch, slab_scr, tc, nch)
        for j in range(nch):
            for a in range((k * nch + j) * per_chunk, (k * nch + j + 1) * per_chunk):
                t, kk = divmod(a, TOP_K)
                row_copy(pl.multiple_of(dest_next_ref[a] * nch, nch), 1 - slot,
                         (kk * tc + t) * nch).start(priority=a % 2)
            out_ref[:, j * LANES:(j + 1) * LANES] += wk * _slab_chunk(slab_scr, j, tc, nch)
    out_ref[...] = _rms(out_ref[...], gf_ref[...])

    @pl.when(i == pl.num_programs(0) - 1)
    def _():
        drain(1 - slot)


def _combine(dest_flat, wtok, h1, vp, wsg, wsu, wsd, gf, ys, tc):
    T, D = h1.shape
    FF = wsg.shape[1]
    nch = D // LANES
    const = lambda i: (0, 0)
    n_steps = T // tc
    return pl.pallas_call(
        functools.partial(_combine_kernel, tc=tc, nch=nch),
        grid=(n_steps,),
        in_specs=[
            pl.BlockSpec((tc * TOP_K,), lambda i: (i,), memory_space=pltpu.SMEM),
            pl.BlockSpec((tc * TOP_K,), lambda i: (jnp.minimum(i + 1, n_steps - 1),), memory_space=pltpu.SMEM),
            pl.BlockSpec((tc, TOP_K), lambda i: (i, 0)),
            pl.BlockSpec((tc, D), lambda i: (i, 0)),
            pl.BlockSpec((tc * nch, LANES), lambda i: (i, 0)),
            pl.BlockSpec((D, FF), const),
            pl.BlockSpec((D, FF), const),
            pl.BlockSpec((FF, D), const),
            pl.BlockSpec((1, D), const),
            pl.BlockSpec(memory_space=pl.ANY),
        ],
        out_specs=pl.BlockSpec((tc, D), lambda i: (i, 0)),
        out_shape=jax.ShapeDtypeStruct((T, D), F32),
        scratch_shapes=[pltpu.VMEM((2, TOP_K * tc * nch, LANES), BF16),
                        pltpu.VMEM((2, tc * nch // 2, LANES), F32), pltpu.SemaphoreType.DMA((2,))],
        compiler_params=_cparams(1),
        name="combine",
    )(dest_flat, dest_flat, wtok, h1, vp, wsg, wsu, wsd, gf, ys)


def _layer(h, attn_norm_g, w_in, forget_bias, sinks, w_branch_swa, w_branch_fox, w_out, ffn_norm_g,
           w_router, router_bias, w_exp_gate, w_exp_up, w_exp_down, w_sh_gate, w_sh_up, w_sh_down, out_g):
    B, S, D = h.shape
    T = B * S
    E = w_router.shape[1]
    x2 = h.reshape(T, D)

    o = np.cumsum([0, SWA_Q_W, SWA_KV_W, SWA_KV_W, FOX_W, FOX_W, FOX_W, FORGET_W, D, D])
    aq, ak, av, fq, fk, fv, fg, gsw, gfx = [w_in[:, o[i]:o[i + 1]] for i in range(9)]

    def dup(w):
        w4 = w.reshape(D, SWA_KV_HEADS, 1, HEAD_DIM)
        return jnp.broadcast_to(w4, (D, SWA_KV_HEADS, 2, HEAD_DIM)).reshape(D, SWA_KV_EXP_W)

    w_main = jnp.concatenate([gsw, gfx, aq, dup(ak), dup(av), fq, fk, fv], axis=1).astype(BF16)
    spread = lambda a: jnp.pad(a[..., None], [(0, 0)] * a.ndim + [(0, FORGET_STRIDE - 1)]).reshape(*a.shape[:-1], LANES)
    w_f = spread(fg).astype(BF16)
    fb = spread(forget_bias.astype(F32).reshape(1, FORGET_W))

    z, f = _inproj(x2, attn_norm_g.reshape(1, D), w_main, w_f, tm=min(1024, T))
    ccol = _forget_cumsum(f, fb, B, S)
    y_swa = _swa(z, sinks.astype(F32), B, S, col0=2 * D)
    y_fox = _fox(z, ccol, B, S, col0=2 * D + SWA_Q_W + 2 * SWA_KV_EXP_W)

    h1, vp, eidx, rank, wgt, cnt = _post(
        y_swa, y_fox, z, x2, w_branch_swa.astype(BF16), w_branch_fox.astype(BF16), w_out.astype(BF16),
        ffn_norm_g.reshape(1, D), w_router.T, router_bias.reshape(E, 1).astype(F32), tm=min(256, T))

    rows = MOE_ROWS
    n_blocks = T * TOP_K // rows + E
    counts = cnt[:, 0].astype(I32)
    padded = (counts + rows - 1) // rows * rows
    pad_end = jnp.cumsum(padded)
    pad_start = pad_end - padded
    experts = jnp.arange(E, dtype=I32)[:, None, None]
    start_of = jnp.sum(jnp.where(eidx[None] == experts, pad_start[:, None, None], 0), axis=0)
    dest_flat = (start_of + rank).T.reshape(T * TOP_K)
    n_used = (pad_end[-1] // rows).astype(I32)
    blk = jnp.arange(n_blocks, dtype=I32)
    blk_exp = jnp.minimum(jnp.sum((pad_end[None, :] <= blk[:, None] * rows).astype(I32), axis=1), E - 1)
    e_ids = jnp.arange(E, dtype=I32)
    owns = counts > 0
    ord_of = jnp.cumsum(owns.astype(I32)) - 1
    n_exp_used = jnp.sum(owns.astype(I32))
    exp_list = jnp.sum(jnp.where((ord_of[None, :] == e_ids[:, None]) & owns[None, :], e_ids[None, :], 0), axis=1)
    blk_ord = jnp.sum(jnp.where(blk_exp[:, None] == e_ids[None, :], ord_of[None, :], 0), axis=1)

    last_of_ord = jnp.sum(jnp.where((ord_of[None, :] == e_ids[:, None]) & owns[None, :],
                                    (pad_end // rows - 1)[None, :], 0), axis=1)
    zi = jnp.arange(2 * E, dtype=I32)
    zero_blocks = jnp.where(zi < n_exp_used, jnp.concatenate([last_of_ord, last_of_ord]),
                            n_used + zi - n_exp_used)
    n_zero = n_exp_used + n_blocks - n_used

    xs = _dispatch(zero_blocks, n_zero.reshape(1), dest_flat, vp, n_blocks * rows, td=min(256, T), blk_rows=rows)
    ys = _experts(blk_ord, exp_list, n_used.reshape(1), n_exp_used.reshape(1), xs,
                  w_exp_gate, w_exp_up, w_exp_down, rows)
    out = _combine(dest_flat, wgt.T, h1, vp, w_sh_gate.astype(BF16), w_sh_up.astype(BF16),
                   w_sh_down.astype(BF16), out_g.reshape(1, D), ys, tc=min(256, T))
    return out.reshape(B, S, D)


def kernel(x, attn_norm_g, w_in, forget_bias, sinks, w_branch_swa, w_branch_fox, w_out, ffn_norm_g,
           w_router, router_bias, w_exp_gate, w_exp_up, w_exp_down, w_sh_gate, w_sh_up, w_sh_down,
           final_norm_g):
    depth = w_in.shape[0]
    assert depth == 1, "the final norm is fused into the last layer's combine"
    return _layer(x, attn_norm_g[0], w_in[0], forget_bias[0], sinks[0], w_branch_swa[0], w_branch_fox[0],
                  w_out[0], ffn_norm_g[0], w_router[0], router_bias[0], w_exp_gate[0], w_exp_up[0],
                  w_exp_down[0], w_sh_gate[0], w_sh_up[0], w_sh_down[0], final_norm_g)
```

```python
import functools

import numpy as np
import jax
import jax.numpy as jnp
from jax import lax
from jax.experimental import pallas as pl
from jax.experimental.pallas import tpu as pltpu

F32 = jnp.float32
BF16 = jnp.bfloat16
I32 = jnp.int32

HEAD_DIM = 64
SWA_Q_HEADS = 16
SWA_KV_HEADS = 4
FOX_HEADS = 16
WINDOW = 128
BLOCK = 128
N_GROUPS = 8
TOPK_GROUPS = 4
TOP_K = 8
ROUTED_SCALE = 2.5
RMS_EPS = 1e-6
LANES = 128
SWA_Q_W = SWA_Q_HEADS * HEAD_DIM
SWA_KV_W = SWA_KV_HEADS * HEAD_DIM
FOX_W = FOX_HEADS * HEAD_DIM
FORGET_W = FOX_HEADS
FORGET_STRIDE = 8
SWA_KV_EXP_W = 2 * SWA_KV_W
Z_TILE = 1024
FOX_TQ = 128
FOX_TK = 256
FOX_SCORE_LEAD = 2
FOX_PAIRS_PER_LOOP = 8
MOE_ROWS = 256
POST_COL_CHUNK = 512
WEIGHT_SLOTS = 3
VMEM_LIMIT = 56 * 1024 * 1024
HI = lax.Precision.HIGHEST
NEG_INF = float("-inf")


def _cparams(n_axes):
    return pltpu.CompilerParams(dimension_semantics=("arbitrary",) * n_axes,
                                vmem_limit_bytes=VMEM_LIMIT)


def _rms(x, g):
    return x * lax.rsqrt(jnp.mean(x * x, axis=-1, keepdims=True) + RMS_EPS) * g


def _store_slabs(ref, scr, x):
    rows = x.shape[0]
    n = x.shape[1] // LANES
    hn = n // 2
    for j in range(n):
        scr[j // hn, pl.ds(j % hn, rows, stride=hn), :] = x[:, j * LANES:(j + 1) * LANES]
    halves = [scr[i].reshape(rows, 1, hn, LANES) for i in range(2)]
    ref[...] = jnp.concatenate(halves, axis=1).reshape(rows * n, LANES).astype(BF16)


def _fill_slab_scratch(ref, base, scr, rows, n):
    hn = n // 2
    xf = ref[pl.ds(base, rows * n), :].astype(F32).reshape(rows, 2, hn, LANES)
    for i in range(2):
        scr[i] = xf[:, i].reshape(rows * hn, LANES)


def _slab_chunk(scr, j, rows, n):
    hn = n // 2
    return scr[j // hn, pl.ds(j % hn, rows, stride=hn), :]


def _load_slabs(ref, base, scr, rows, n):
    _fill_slab_scratch(ref, base, scr, rows, n)
    return [_slab_chunk(scr, j, rows, n) for j in range(n)]


def _inproj_kernel(x_ref, g_ref, w_ref, wf_ref, z_ref, f_ref, u_scr, *, n_gate_tiles):
    j = pl.program_id(1)

    @pl.when(j == 0)
    def _():
        u = _rms(x_ref[...], g_ref[...]).astype(BF16)
        u_scr[...] = u
        f_ref[...] = jnp.dot(u, wf_ref[...], preferred_element_type=F32)

    z = jnp.dot(u_scr[...], w_ref[...], preferred_element_type=F32)

    @pl.when(j < n_gate_tiles)
    def _():
        z_ref[...] = jax.nn.sigmoid(z).astype(BF16)

    @pl.when(j >= n_gate_tiles)
    def _():
        z_ref[...] = z.astype(BF16)


def _inproj(x2, g, w_main, w_f, tm):
    T, D = x2.shape
    ZW = w_main.shape[1]
    return pl.pallas_call(
        functools.partial(_inproj_kernel, n_gate_tiles=2 * D // Z_TILE),
        grid=(T // tm, ZW // Z_TILE),
        in_specs=[
            pl.BlockSpec((tm, D), lambda i, j: (i, 0)),
            pl.BlockSpec((1, D), lambda i, j: (0, 0)),
            pl.BlockSpec((D, Z_TILE), lambda i, j: (0, j)),
            pl.BlockSpec((D, LANES), lambda i, j: (0, 0)),
        ],
        out_specs=[
            pl.BlockSpec((tm, Z_TILE), lambda i, j: (i, j)),
            pl.BlockSpec((tm, LANES), lambda i, j: (i, 0)),
        ],
        out_shape=[jax.ShapeDtypeStruct((T, ZW), BF16), jax.ShapeDtypeStruct((T, LANES), F32)],
        scratch_shapes=[pltpu.VMEM((tm, D), BF16)],
        compiler_params=_cparams(2),
        name="inproj",
    )(x2, g, w_main, w_f)


def _fcum_kernel(f_ref, b_ref, ccol_ref, carry_ref):
    @pl.when(pl.program_id(1) == 0)
    def _():
        carry_ref[...] = jnp.zeros_like(carry_ref)

    a = f_ref[...] + b_ref[...]
    ls = jnp.minimum(a, 0.0) - jnp.log1p(jnp.exp(-jnp.abs(a)))
    n = ls.shape[0]
    tri = (lax.broadcasted_iota(I32, (n, n), 1) <= lax.broadcasted_iota(I32, (n, n), 0)).astype(F32)
    c = jnp.dot(tri, ls, precision=HI, preferred_element_type=F32) + carry_ref[...]
    ccol_ref[...] = c
    carry_ref[...] = c[n - 1:n, :]


def _forget_cumsum(f, bias, B, S):
    T = B * S
    nb = S // FOX_TK
    return pl.pallas_call(
        _fcum_kernel,
        grid=(B, nb),
        in_specs=[
            pl.BlockSpec((FOX_TK, LANES), lambda b, j: (b * nb + j, 0)),
            pl.BlockSpec((1, LANES), lambda b, j: (0, 0)),
        ],
        out_specs=pl.BlockSpec((FOX_TK, LANES), lambda b, j: (b * nb + j, 0)),
        out_shape=jax.ShapeDtypeStruct((T, LANES), F32),
        scratch_shapes=[pltpu.VMEM((1, LANES), F32)],
        compiler_params=_cparams(2),
        name="forget_cumsum",
    )(f, bias)


def _swa_kernel(sink_ref, q_ref, kp_ref, kc_ref, vp_ref, vc_ref, y_ref, *, slopes):
    qi = pl.program_id(1)
    kk = jnp.concatenate([kp_ref[...], kc_ref[...]], axis=0)
    vv = jnp.concatenate([vp_ref[...], vc_ref[...]], axis=0)
    row = lax.broadcasted_iota(I32, (BLOCK, 2 * BLOCK), 0)
    col = lax.broadcasted_iota(I32, (BLOCK, 2 * BLOCK), 1)
    dist = row + BLOCK - col
    valid = (dist >= 0) & (dist < WINDOW) & ((qi > 0) | (col >= BLOCK))
    distf = dist.astype(F32)
    lo_half = lax.broadcasted_iota(I32, (BLOCK, LANES), 1) < HEAD_DIM
    group = SWA_Q_HEADS // SWA_KV_HEADS
    for pair in range(SWA_Q_HEADS // 2):
        q2 = q_ref[:, pair * LANES:(pair + 1) * LANES]
        g = (2 * pair) // group
        kg = kk[:, g * LANES:(g + 1) * LANES]
        vg = vv[:, g * LANES:(g + 1) * LANES]
        outs = []
        for half in range(2):
            h = 2 * pair + half
            keep = lo_half if half == 0 else jnp.logical_not(lo_half)
            qm = jnp.where(keep, q2, jnp.zeros_like(q2)) * jnp.asarray(HEAD_DIM ** -0.5, BF16)
            s = lax.dot_general(qm, kg, (((1,), (1,)), ((), ())), preferred_element_type=F32)
            s = jnp.where(valid, s - slopes[h] * distf, NEG_INF)
            sink = sink_ref[h]
            m = jnp.maximum(jnp.max(s, axis=-1, keepdims=True), sink)
            p = jnp.exp(s - m)
            denom = jnp.sum(p, axis=-1, keepdims=True) + jnp.exp(sink - m)
            o = jnp.dot(p.astype(BF16), vg, preferred_element_type=F32)
            outs.append(o / denom)
        y_ref[:, pair * LANES:(pair + 1) * LANES] = jnp.where(lo_half, outs[0], outs[1]).astype(BF16)


def _swa(z, sinks, B, S, col0):
    T = B * S
    nq = S // BLOCK
    qb = col0 // SWA_Q_W
    kb = (col0 + SWA_Q_W) // SWA_KV_EXP_W
    vb = kb + 1
    slopes = tuple(2.0 ** (-8.0 * (h + 1) / SWA_Q_HEADS) for h in range(SWA_Q_HEADS))
    cur = lambda b, i: b * nq + i
    prev = lambda b, i: jnp.maximum(b * nq + i - 1, 0)
    return pl.pallas_call(
        functools.partial(_swa_kernel, slopes=slopes),
        grid=(B, nq),
        in_specs=[
            pl.BlockSpec(memory_space=pltpu.SMEM),
            pl.BlockSpec((BLOCK, SWA_Q_W), lambda b, i: (cur(b, i), qb)),
            pl.BlockSpec((BLOCK, SWA_KV_EXP_W), lambda b, i: (prev(b, i), kb)),
            pl.BlockSpec((BLOCK, SWA_KV_EXP_W), lambda b, i: (cur(b, i), kb)),
            pl.BlockSpec((BLOCK, SWA_KV_EXP_W), lambda b, i: (prev(b, i), vb)),
            pl.BlockSpec((BLOCK, SWA_KV_EXP_W), lambda b, i: (cur(b, i), vb)),
        ],
        out_specs=pl.BlockSpec((BLOCK, SWA_Q_W), lambda b, i: (cur(b, i), 0)),
        out_shape=jax.ShapeDtypeStruct((T, SWA_Q_W), BF16),
        compiler_params=_cparams(2),
        name="swa",
    )(sinks, z, z, z, z, z)


def _split3(x):
    hi = x.astype(BF16).astype(F32)
    r1 = x - hi
    mid = r1.astype(BF16).astype(F32)
    lo = (r1 - mid).astype(BF16).astype(F32)
    return hi, mid, lo


def _split3_adjacent(c, lane):
    hi, mid, lo = _split3(jnp.where((lane & (FORGET_STRIDE - 1)) == 0, c, 0.0))
    return hi + pltpu.roll(mid, 1, axis=1) + pltpu.roll(lo, 2, axis=1)


def _aug_lanes(lane, terms, h, val0, one0):
    ones = jnp.where((lane >= one0) & (lane < one0 + 3), 1.0, 0.0)
    moved = pltpu.roll(terms, (val0 - h * FORGET_STRIDE) % LANES, axis=1)
    return jnp.where((lane >= val0) & (lane < val0 + 3), moved, ones)


def _fox_kernel(q_ref, k_ref, v_ref, c_ref, y_ref, ka_ref, va_ref):
    qi = pl.program_id(1)
    S = k_ref.shape[0]
    lane_k = lax.broadcasted_iota(I32, (FOX_TK, LANES), 1)
    lane_q = lax.broadcasted_iota(I32, (FOX_TQ, LANES), 1)

    def head_lanes(h):
        return slice(h * LANES, (h + 1) * LANES)

    def spare0(h):
        return HEAD_DIM if h % 2 == 0 else 0

    def keep(lane, h):
        return (lane < HEAD_DIM) if h % 2 == 0 else (lane >= HEAD_DIM)

    @pl.when(qi == 0)
    def _():
        def build(r, carry):
            r0 = pl.multiple_of(r * FOX_TK, FOX_TK)
            parts = _split3_adjacent(-c_ref[pl.ds(r0, FOX_TK), :], lane_k)
            for pair in range(FOX_HEADS // 2):
                kp = k_ref[pl.ds(r0, FOX_TK), pair * LANES:(pair + 1) * LANES].astype(F32)
                vp = v_ref[pl.ds(r0, FOX_TK), pair * LANES:(pair + 1) * LANES]
                for h in (2 * pair, 2 * pair + 1):
                    extra = _aug_lanes(lane_k, parts, h, spare0(h), spare0(h) + 3)
                    ka_ref[pl.ds(r0, FOX_TK), head_lanes(h)] = jnp.where(keep(lane_k, h), kp, extra).astype(BF16)
                    va_ref[pl.ds(r0, FOX_TK), head_lanes(h)] = jnp.where(keep(lane_k, h), vp, jnp.ones_like(vp))
            return carry

        lax.fori_loop(0, S // FOX_TK, build, 0)

    n_full = qi * FOX_TQ // FOX_TK
    row = qi * FOX_TQ + lax.broadcasted_iota(I32, (FOX_TQ, FOX_TK), 0)
    col = lax.broadcasted_iota(I32, (FOX_TQ, FOX_TK), 1)
    ct_parts = _split3_adjacent(c_ref[pl.ds(pl.multiple_of(qi * FOX_TQ, FOX_TQ), FOX_TQ), :], lane_q)

    hpl = 2 * FOX_PAIRS_PER_LOOP
    for h0 in range(0, FOX_HEADS, hpl):
        heads = list(range(h0, h0 + hpl))
        qas = []
        for h in heads:
            q2 = q_ref[:, (h // 2) * LANES:(h // 2 + 1) * LANES].astype(F32) * (HEAD_DIM ** -0.5)
            extra = _aug_lanes(lane_q, ct_parts, h, spare0(h) + 3, spare0(h))
            qas.append(jnp.where(keep(lane_q, h), q2, extra).astype(BF16))

        def step(j, carry, masked, heads=heads, qas=qas):
            k0 = pl.multiple_of(j * FOX_TK, FOX_TK)

            def scores(i):
                return lax.dot_general(qas[i], ka_ref[pl.ds(k0, FOX_TK), head_lanes(heads[i])],
                                       (((1,), (1,)), ((), ())), preferred_element_type=F32)

            ready = [scores(i) for i in range(min(FOX_SCORE_LEAD, len(heads)))]
            out = []
            for i, h in enumerate(heads):
                m, acc = carry[2 * i:2 * i + 2]
                if i + FOX_SCORE_LEAD < len(heads):
                    ready.append(scores(i + FOX_SCORE_LEAD))
                s = ready.pop(0)
                if masked:
                    s = jnp.where(col + j * FOX_TK <= row, s, NEG_INF)
                m_new = jnp.maximum(m, jnp.max(s, axis=-1, keepdims=True))
                p = jnp.exp(s - m_new).astype(BF16)
                acc = jnp.exp(m - m_new) * acc + jnp.dot(p, va_ref[pl.ds(k0, FOX_TK), head_lanes(h)],
                                                         preferred_element_type=F32)
                out += [m_new, acc]
            return tuple(out)

        init = (jnp.full((FOX_TQ, 1), -1e30, F32), jnp.zeros((FOX_TQ, LANES), F32)) * hpl
        carry = lax.fori_loop(0, n_full, functools.partial(step, masked=False), init)
        carry = step(n_full, carry, True)
        for i in range(0, hpl, 2):
            acc_lo, acc_hi = carry[2 * i + 1], carry[2 * i + 3]
            o_lo = acc_lo / acc_lo[:, LANES - 1:LANES]
            o_hi = acc_hi / acc_hi[:, 0:1]
            pair = heads[i] // 2
            y_ref[:, pair * LANES:(pair + 1) * LANES] = jnp.where(lane_q < HEAD_DIM, o_lo, o_hi).astype(BF16)


def _fox(z, ccol, B, S, col0):
    T = B * S
    nq = S // FOX_TQ
    qb = col0 // FOX_W
    return pl.pallas_call(
        _fox_kernel,
        grid=(B, nq),
        in_specs=[
            pl.BlockSpec((FOX_TQ, FOX_W), lambda b, i: (b * nq + i, qb)),
            pl.BlockSpec((S, FOX_W), lambda b, i: (b, qb + 1)),
            pl.BlockSpec((S, FOX_W), lambda b, i: (b, qb + 2)),
            pl.BlockSpec((S, LANES), lambda b, i: (b, 0)),
        ],
        out_specs=pl.BlockSpec((FOX_TQ, FOX_W), lambda b, i: (b * nq + i, 0)),
        out_shape=jax.ShapeDtypeStruct((T, FOX_W), BF16),
        scratch_shapes=[pltpu.VMEM((S, FOX_HEADS * LANES), BF16), pltpu.VMEM((S, FOX_HEADS * LANES), BF16)],
        compiler_params=_cparams(2),
        name="fox",
    )(z, z, z, ccol)


def _first_argmax(vals, iota, n):
    m = jnp.max(vals, axis=0, keepdims=True)
    idx = jnp.min(jnp.where(vals == m, iota, n), axis=0, keepdims=True)
    return m, idx


def _route(logits_t, rb, tick=lambda: None):
    E, tm = logits_t.shape
    gs = E // N_GROUPS
    scores = jax.nn.sigmoid(logits_t)
    biased = scores + rb
    iota_g = lax.broadcasted_iota(I32, (gs, tm), 0)
    iota_n = lax.broadcasted_iota(I32, (N_GROUPS, tm), 0)
    grp = jnp.zeros((N_GROUPS, tm), F32)
    for g in range(N_GROUPS):
        blk = biased[g * gs:(g + 1) * gs, :]
        m1, i1 = _first_argmax(blk, iota_g, gs)
        m2 = jnp.max(jnp.where(iota_g == i1, NEG_INF, blk), axis=0, keepdims=True)
        grp = jnp.where(iota_n == g, m1 + m2, grp)
    tick()
    gsel = jnp.zeros((N_GROUPS, tm), jnp.bool_)
    cur = grp
    for _ in range(TOPK_GROUPS):
        _, gi = _first_argmax(cur, iota_n, N_GROUPS)
        oh = iota_n == gi
        gsel = gsel | oh
        cur = jnp.where(oh, NEG_INF, cur)
    tick()
    gself = gsel.astype(F32)
    emask = jnp.concatenate(
        [jnp.broadcast_to(gself[g:g + 1, :], (gs, tm)) for g in range(N_GROUPS)], axis=0) > 0.5
    cur = jnp.where(emask, biased, NEG_INF)
    iota_e = lax.broadcasted_iota(I32, (E, tm), 0)
    sel = jnp.zeros((E, tm), jnp.bool_)
    picks = []
    for _ in range(TOP_K):
        _, ei = _first_argmax(cur, iota_e, E)
        oh = iota_e == ei
        sel = sel | oh
        cur = jnp.where(oh, NEG_INF, cur)
        picks.append(ei)
        tick()
    w = jnp.where(sel, scores, 0.0)
    w = w / jnp.sum(w, axis=0, keepdims=True) * ROUTED_SCALE
    return sel, w, picks, iota_e


def _post_kernel(ysw_ref, yfx_ref, gsw_ref, gfx_ref, x_ref, wsw_ref, wfx_ref, wo_ref, g2_ref,
                 wrt_ref, rb_ref, h1_ref, vp_ref, eidx_ref, rank_ref, wgt_ref, cnt_ref,
                 carry_ref, slab_scr, vprev_ref):
    i = pl.program_id(0)

    @pl.when(i == 0)
    def _():
        carry_ref[...] = jnp.zeros_like(carry_ref)
        vprev_ref[...] = jnp.zeros_like(vprev_ref)

    tm, D = x_ref.shape
    n_col = D // POST_COL_CHUNK

    def mixer():
        ysw, yfx = ysw_ref[...], yfx_ref[...]
        parts = []
        for c in range(n_col):
            cols = slice(c * POST_COL_CHUNK, (c + 1) * POST_COL_CHUNK)
            a = jnp.dot(ysw, wsw_ref[:, cols], preferred_element_type=F32)
            yield
            b = jnp.dot(yfx, wfx_ref[:, cols], preferred_element_type=F32)
            parts.append((gsw_ref[:, cols].astype(F32) * a + gfx_ref[:, cols].astype(F32) * b).astype(BF16))
            yield
        mix = jnp.concatenate(parts, axis=1)
        for c in range(n_col):
            cols = slice(c * POST_COL_CHUNK, (c + 1) * POST_COL_CHUNK)
            h1_ref[:, cols] = x_ref[:, cols] + jnp.dot(mix, wo_ref[:, cols], preferred_element_type=F32)
            yield

    chunks = mixer()
    tick = lambda: next(chunks, None)

    logits_t = lax.dot_general(wrt_ref[...], vprev_ref[...], (((1,), (1,)), ((), ())),
                               precision=HI, preferred_element_type=F32)
    tick()
    sel, w, picks, iota_e = _route(logits_t, rb_ref[...], tick)
    live = (i > 0).astype(F32)

    before = (lax.broadcasted_iota(I32, (tm, tm), 0) < lax.broadcasted_iota(I32, (tm, tm), 1)).astype(BF16)
    rank = jnp.dot(sel.astype(BF16), before, preferred_element_type=F32) + carry_ref[...]
    carry_ref[...] = carry_ref[...] + live * jnp.sum(sel.astype(F32), axis=1, keepdims=True)
    cnt_ref[...] = carry_ref[...]
    tick()

    iota_k = lax.broadcasted_iota(I32, (TOP_K, tm), 0)
    eidx = jnp.zeros((TOP_K, tm), I32)
    rnk = jnp.zeros((TOP_K, tm), F32)
    wgt = jnp.zeros((TOP_K, tm), F32)
    for k, ei in enumerate(picks):
        oh = iota_e == ei
        eidx = jnp.where(iota_k == k, ei, eidx)
        rnk = jnp.where(iota_k == k, jnp.sum(jnp.where(oh, rank, 0.0), axis=0, keepdims=True), rnk)
        wgt = jnp.where(iota_k == k, jnp.sum(jnp.where(oh, w, 0.0), axis=0, keepdims=True), wgt)
        tick()
    eidx_ref[...] = eidx
    rank_ref[...] = rnk.astype(I32)
    wgt_ref[...] = wgt

    for _ in chunks:
        pass
    v = _rms(h1_ref[...], g2_ref[...])
    _store_slabs(vp_ref, slab_scr, v)
    vprev_ref[...] = v


def _post(ysw, yfx, z, x2, wsw, wfx, wo, g2, wrt, rb, tm):
    T, D = x2.shape
    E = wrt.shape[0]
    nch = D // LANES
    n = T // tm
    const = lambda i: (0, 0)
    cur = lambda i: jnp.minimum(i, n - 1)
    prev = lambda i: jnp.maximum(i - 1, 0)
    return pl.pallas_call(
        _post_kernel,
        grid=(n + 1,),
        in_specs=[
            pl.BlockSpec((tm, SWA_Q_W), lambda i: (cur(i), 0)),
            pl.BlockSpec((tm, FOX_W), lambda i: (cur(i), 0)),
            pl.BlockSpec((tm, D), lambda i: (cur(i), 0)),
            pl.BlockSpec((tm, D), lambda i: (cur(i), 1)),
            pl.BlockSpec((tm, D), lambda i: (cur(i), 0)),
            pl.BlockSpec((SWA_Q_W, D), const),
            pl.BlockSpec((FOX_W, D), const),
            pl.BlockSpec((D, D), const),
            pl.BlockSpec((1, D), const),
            pl.BlockSpec((E, D), const),
            pl.BlockSpec((E, 1), const),
        ],
        out_specs=[
            pl.BlockSpec((tm, D), lambda i: (cur(i), 0)),
            pl.BlockSpec((tm * nch, LANES), lambda i: (cur(i), 0)),
            pl.BlockSpec((TOP_K, tm), lambda i: (0, prev(i))),
            pl.BlockSpec((TOP_K, tm), lambda i: (0, prev(i))),
            pl.BlockSpec((TOP_K, tm), lambda i: (0, prev(i))),
            pl.BlockSpec((E, 1), const),
        ],
        out_shape=[
            jax.ShapeDtypeStruct((T, D), F32),
            jax.ShapeDtypeStruct((T * nch, LANES), BF16),
            jax.ShapeDtypeStruct((TOP_K, T), I32),
            jax.ShapeDtypeStruct((TOP_K, T), I32),
            jax.ShapeDtypeStruct((TOP_K, T), F32),
            jax.ShapeDtypeStruct((E, 1), F32),
        ],
        scratch_shapes=[pltpu.VMEM((E, 1), F32), pltpu.VMEM((2, tm * nch // 2, LANES), F32),
                        pltpu.VMEM((tm, D), F32)],
        compiler_params=_cparams(1),
        name="post_mixer",
    )(ysw, yfx, z, z, x2, wsw, wfx, wo, g2, wrt, rb)


def _dispatch_kernel(zblk_ref, nz_ref, dest_ref, vp_ref, xs_out, zeros, sem, zsem, *, td, nch, blk_rows):
    @pl.when(pl.program_id(0) == 0)
    def _():
        zeros[...] = jnp.zeros_like(zeros)

        def zero_copy(blk):
            dst = pl.multiple_of(blk * blk_rows * nch, blk_rows * nch)
            return pltpu.make_async_copy(zeros, xs_out.at[pl.ds(dst, blk_rows * nch), :], zsem)

        def zstart(i, c):
            zero_copy(zblk_ref[i]).start()
            return c

        def zwait(i, c):
            zero_copy(0).wait()
            return c

        lax.fori_loop(0, nz_ref[0], zstart, 0)
        lax.fori_loop(0, nz_ref[0], zwait, 0)

    def row_copy(src_row, dst_row):
        return pltpu.make_async_copy(vp_ref.at[pl.ds(src_row, nch), :],
                                     xs_out.at[pl.ds(dst_row, nch), :], sem)

    def issue(t, c):
        src = pl.multiple_of(t * nch, nch)
        for k in range(TOP_K):
            row_copy(src, pl.multiple_of(dest_ref[t * TOP_K + k] * nch, nch)).start(priority=k % 2)
        return c

    lax.fori_loop(0, td, issue, 0)

    def drain(t, c):
        for k in range(TOP_K):
            row_copy(0, 0).wait()
        return c

    lax.fori_loop(0, td, drain, 0)


def _dispatch(zero_blocks, n_zero, dest_flat, vp, n_rows, td, blk_rows):
    nch = vp.shape[0] * TOP_K // dest_flat.shape[0]
    T = vp.shape[0] // nch
    return pl.pallas_call(
        functools.partial(_dispatch_kernel, td=td, nch=nch, blk_rows=blk_rows),
        grid_spec=pltpu.PrefetchScalarGridSpec(
            num_scalar_prefetch=2,
            grid=(T // td,),
            in_specs=[
                pl.BlockSpec((td * TOP_K,), lambda i, zb, nz: (i,), memory_space=pltpu.SMEM),
                pl.BlockSpec((td * nch, LANES), lambda i, zb, nz: (i, 0)),
            ],
            out_specs=pl.BlockSpec(memory_space=pl.ANY),
            scratch_shapes=[pltpu.VMEM((blk_rows * nch, LANES), BF16), pltpu.SemaphoreType.DMA(()),
                            pltpu.SemaphoreType.DMA(())],
        ),
        out_shape=jax.ShapeDtypeStruct((n_rows * nch, LANES), BF16),
        compiler_params=_cparams(1),
        name="dispatch",
    )(zero_blocks, n_zero, dest_flat, vp)


def _experts_kernel(bo_ref, el_ref, nu_ref, ne_ref, x_ref, wg_hbm, wu_hbm, wd_hbm, y_ref,
                    wg_f, wu_f, wd_f, wg_s, wu_s, wd_s, slab_scr, sems, *, rows, nch):
    b = pl.program_id(0)
    weights = ((wg_hbm, wg_f, wg_s), (wu_hbm, wu_f, wu_s), (wd_hbm, wd_f, wd_s))

    def fetch(o, slot):
        e = el_ref[o]
        return [pltpu.make_async_copy(w_hbm.at[e], w_f.at[slot], sems.at[slot, i])
                for i, (w_hbm, w_f, _) in enumerate(weights)]

    def start_fetch(o):
        @pl.when(o < ne_ref[0])
        def _():
            for copy in fetch(o, o % WEIGHT_SLOTS):
                copy.start(priority=1)

    @pl.when(b == 0)
    def _():
        for o in range(WEIGHT_SLOTS - 1):
            start_fetch(o)

    @pl.when(b < nu_ref[0])
    def _():
        o = bo_ref[b]
        slot = o % WEIGHT_SLOTS

        @pl.when((b == 0) | (o != bo_ref[jnp.maximum(b - 1, 0)]))
        def _():
            start_fetch(o + WEIGHT_SLOTS - 1)

            for copy, (_, w_f, w_s) in zip(fetch(o, slot), weights):
                copy.wait()
                w_s[...] = w_f[slot].astype(BF16)

        x = jnp.concatenate([c.astype(BF16) for c in _load_slabs(x_ref, 0, slab_scr, rows, nch)], axis=1)
        hg = jnp.dot(x, wg_s[...], preferred_element_type=F32)
        hu = jnp.dot(x, wu_s[...], preferred_element_type=F32)
        act = (jax.nn.silu(hg) * hu).astype(BF16)
        y = jnp.dot(act, wd_s[...], preferred_element_type=F32)
        _store_slabs(y_ref, slab_scr, y)

    @pl.when(b >= nu_ref[0])
    def _():
        y_ref[...] = jnp.zeros_like(y_ref)


def _experts(blk_ord, exp_list, n_used, n_exp_used, xs, wg, wu, wd, rows):
    E, D, FF = wg.shape
    nch = D // LANES
    nblk = xs.shape[0] // (rows * nch)
    xmap = lambda b, bo, el, nu, ne: (jnp.maximum(jnp.minimum(b, nu[0] - 1), 0), 0)
    return pl.pallas_call(
        functools.partial(_experts_kernel, rows=rows, nch=nch),
        grid_spec=pltpu.PrefetchScalarGridSpec(
            num_scalar_prefetch=4,
            grid=(nblk,),
            in_specs=[
                pl.BlockSpec((rows * nch, LANES), xmap),
                pl.BlockSpec(memory_space=pl.ANY),
                pl.BlockSpec(memory_space=pl.ANY),
                pl.BlockSpec(memory_space=pl.ANY),
            ],
            out_specs=pl.BlockSpec((rows * nch, LANES), lambda b, bo, el, nu, ne: (b, 0)),
            scratch_shapes=[pltpu.VMEM((WEIGHT_SLOTS, D, FF), F32), pltpu.VMEM((WEIGHT_SLOTS, D, FF), F32),
                            pltpu.VMEM((WEIGHT_SLOTS, FF, D), F32),
                            pltpu.VMEM((D, FF), BF16), pltpu.VMEM((D, FF), BF16), pltpu.VMEM((FF, D), BF16),
                            pltpu.VMEM((2, rows * nch // 2, LANES), F32),
                            pltpu.SemaphoreType.DMA((WEIGHT_SLOTS, 3))],
        ),
        out_shape=jax.ShapeDtypeStruct(xs.shape, BF16),
        compiler_params=_cparams(1),
        name="experts",
    )(blk_ord, exp_list, n_used, n_exp_used, xs, wg, wu, wd)


def _combine_kernel(dest_ref, dest_next_ref, wtok_ref, h1_ref, vp_ref, wsg_ref, wsu_ref, wsd_ref, gf_ref,
                    ys_hbm, out_ref, buf, slab_scr, sems, *, tc, nch):
    i = pl.program_id(0)
    slot = i % 2

    def row_copy(src_row, s, dst_row):
        return pltpu.make_async_copy(ys_hbm.at[pl.ds(src_row, nch), :],
                                     buf.at[s].at[pl.ds(dst_row, nch), :], sems.at[s])

    def gather(idx_ref, s):
        def issue(t, c):
            for k in range(TOP_K):
                row_copy(pl.multiple_of(idx_ref[t * TOP_K + k] * nch, nch), s,
                         pl.multiple_of((k * tc + t) * nch, nch)).start(priority=k % 2)
            return c

        lax.fori_loop(0, tc, issue, 0)

    @pl.when(i == 0)
    def _():
        gather(dest_ref, 0)

    def issue_next(lo, hi):
        for a in range(lo, hi):
            t, kk = divmod(a, TOP_K)
            row_copy(pl.multiple_of(dest_next_ref[a] * nch, nch), 1 - slot,
                     (kk * tc + t) * nch).start(priority=a % 2)

    early = tc * TOP_K // 4
    _fill_slab_scratch(vp_ref, 0, slab_scr, tc, nch)
    xs = []
    for j in range(nch):
        issue_next(j * early // nch, (j + 1) * early // nch)
        xs.append(_slab_chunk(slab_scr, j, tc, nch).astype(BF16))
    x = jnp.concatenate(xs, axis=1)
    hg = jnp.dot(x, wsg_ref[...], preferred_element_type=F32)
    hu = jnp.dot(x, wsu_ref[...], preferred_element_type=F32)
    act = (jax.nn.silu(hg) * hu).astype(BF16)
    out_ref[...] = h1_ref[...] + jnp.dot(act, wsd_ref[...], preferred_element_type=F32)

    def drain(s):
        def body(t, c):
            for k in range(TOP_K):
                row_copy(0, s, 0).wait()
            return c

        lax.fori_loop(0, tc, body, 0)

    drain(slot)

    rows = buf.at[slot]
    per_chunk = (tc * TOP_K - early) // (TOP_K * nch)
    for k in range(TOP_K):
        wk = jnp.broadcast_to(wtok_ref[:, k:k + 1], (tc, LANES))
        _fill_slab_scratch(rows, k * tc * nch, slab_scr, tc, nch)
        for j in range(nch):
            issue_next(early + (k * nch + j) * per_chunk, early + (k * nch + j + 1) * per_chunk)
            out_ref[:, j * LANES:(j + 1) * LANES] += wk * _slab_chunk(slab_scr, j, tc, nch)
    out_ref[...] = _rms(out_ref[...], gf_ref[...])

    @pl.when(i == pl.num_programs(0) - 1)
    def _():
        drain(1 - slot)


def _combine(dest_flat, wtok, h1, vp, wsg, wsu, wsd, gf, ys, tc):
    T, D = h1.shape
    FF = wsg.shape[1]
    nch = D // LANES
    const = lambda i: (0, 0)
    n_steps = T // tc
    return pl.pallas_call(
        functools.partial(_combine_kernel, tc=tc, nch=nch),
        grid=(n_steps,),
        in_specs=[
            pl.BlockSpec((tc * TOP_K,), lambda i: (i,), memory_space=pltpu.SMEM),
            pl.BlockSpec((tc * TOP_K,), lambda i: (jnp.minimum(i + 1, n_steps - 1),), memory_space=pltpu.SMEM),
            pl.BlockSpec((tc, TOP_K), lambda i: (i, 0)),
            pl.BlockSpec((tc, D), lambda i: (i, 0)),
            pl.BlockSpec((tc * nch, LANES), lambda i: (i, 0)),
            pl.BlockSpec((D, FF), const),
            pl.BlockSpec((D, FF), const),
            pl.BlockSpec((FF, D), const),
            pl.BlockSpec((1, D), const),
            pl.BlockSpec(memory_space=pl.ANY),
        ],
        out_specs=pl.BlockSpec((tc, D), lambda i: (i, 0)),
        out_shape=jax.ShapeDtypeStruct((T, D), F32),
        scratch_shapes=[pltpu.VMEM((2, TOP_K * tc * nch, LANES), BF16),
                        pltpu.VMEM((2, tc * nch // 2, LANES), F32), pltpu.SemaphoreType.DMA((2,))],
        compiler_params=_cparams(1),
        name="combine",
    )(dest_flat, dest_flat, wtok, h1, vp, wsg, wsu, wsd, gf, ys)


def _layer(h, attn_norm_g, w_in, forget_bias, sinks, w_branch_swa, w_branch_fox, w_out, ffn_norm_g,
           w_router, router_bias, w_exp_gate, w_exp_up, w_exp_down, w_sh_gate, w_sh_up, w_sh_down, out_g):
    B, S, D = h.shape
    T = B * S
    E = w_router.shape[1]
    x2 = h.reshape(T, D)

    o = np.cumsum([0, SWA_Q_W, SWA_KV_W, SWA_KV_W, FOX_W, FOX_W, FOX_W, FORGET_W, D, D])
    aq, ak, av, fq, fk, fv, fg, gsw, gfx = [w_in[:, o[i]:o[i + 1]] for i in range(9)]

    def dup(w):
        w4 = w.reshape(D, SWA_KV_HEADS, 1, HEAD_DIM)
        return jnp.broadcast_to(w4, (D, SWA_KV_HEADS, 2, HEAD_DIM)).reshape(D, SWA_KV_EXP_W)

    w_main = jnp.concatenate([gsw, gfx, aq, dup(ak), dup(av), fq, fk, fv], axis=1).astype(BF16)
    spread = lambda a: jnp.pad(a[..., None], [(0, 0)] * a.ndim + [(0, FORGET_STRIDE - 1)]).reshape(*a.shape[:-1], LANES)
    w_f = spread(fg).astype(BF16)
    fb = spread(forget_bias.astype(F32).reshape(1, FORGET_W))

    z, f = _inproj(x2, attn_norm_g.reshape(1, D), w_main, w_f, tm=min(1024, T))
    ccol = _forget_cumsum(f, fb, B, S)
    y_swa = _swa(z, sinks.astype(F32), B, S, col0=2 * D)
    y_fox = _fox(z, ccol, B, S, col0=2 * D + SWA_Q_W + 2 * SWA_KV_EXP_W)

    h1, vp, eidx, rank, wgt, cnt = _post(
        y_swa, y_fox, z, x2, w_branch_swa.astype(BF16), w_branch_fox.astype(BF16), w_out.astype(BF16),
        ffn_norm_g.reshape(1, D), w_router.T, router_bias.reshape(E, 1).astype(F32), tm=min(256, T))

    rows = MOE_ROWS
    n_blocks = T * TOP_K // rows + E
    counts = cnt[:, 0].astype(I32)
    padded = (counts + rows - 1) // rows * rows
    pad_end = jnp.cumsum(padded)
    pad_start = pad_end - padded
    experts = jnp.arange(E, dtype=I32)[:, None, None]
    start_of = jnp.sum(jnp.where(eidx[None] == experts, pad_start[:, None, None], 0), axis=0)
    dest_flat = (start_of + rank).T.reshape(T * TOP_K)
    n_used = (pad_end[-1] // rows).astype(I32)
    blk = jnp.arange(n_blocks, dtype=I32)
    blk_exp = jnp.minimum(jnp.sum((pad_end[None, :] <= blk[:, None] * rows).astype(I32), axis=1), E - 1)
    e_ids = jnp.arange(E, dtype=I32)
    owns = counts > 0
    ord_of = jnp.cumsum(owns.astype(I32)) - 1
    n_exp_used = jnp.sum(owns.astype(I32))
    exp_list = jnp.sum(jnp.where((ord_of[None, :] == e_ids[:, None]) & owns[None, :], e_ids[None, :], 0), axis=1)
    blk_ord = jnp.sum(jnp.where(blk_exp[:, None] == e_ids[None, :], ord_of[None, :], 0), axis=1)

    last_of_ord = jnp.sum(jnp.where((ord_of[None, :] == e_ids[:, None]) & owns[None, :],
                                    (pad_end // rows - 1)[None, :], 0), axis=1)
    zi = jnp.arange(2 * E, dtype=I32)
    zero_blocks = jnp.where(zi < n_exp_used, jnp.concatenate([last_of_ord, last_of_ord]),
                            n_used + zi - n_exp_used)
    n_zero = n_exp_used + n_blocks - n_used

    xs = _dispatch(zero_blocks, n_zero.reshape(1), dest_flat, vp, n_blocks * rows, td=min(256, T), blk_rows=rows)
    ys = _experts(blk_ord, exp_list, n_used.reshape(1), n_exp_used.reshape(1), xs,
                  w_exp_gate, w_exp_up, w_exp_down, rows)
    out = _combine(dest_flat, wgt.T, h1, vp, w_sh_gate.astype(BF16), w_sh_up.astype(BF16),
                   w_sh_down.astype(BF16), out_g.reshape(1, D), ys, tc=min(256, T))
    return out.reshape(B, S, D)


def kernel(x, attn_norm_g, w_in, forget_bias, sinks, w_branch_swa, w_branch_fox, w_out, ffn_norm_g,
           w_router, router_bias, w_exp_gate, w_exp_up, w_exp_down, w_sh_gate, w_sh_up, w_sh_down,
           final_norm_g):
    depth = w_in.shape[0]
    assert depth == 1, "the final norm is fused into the last layer's combine"
    return _layer(x, attn_norm_g[0], w_in[0], forget_bias[0], sinks[0], w_branch_swa[0], w_branch_fox[0],
                  w_out[0], ffn_norm_g[0], w_router[0], router_bias[0], w_exp_gate[0], w_exp_up[0],
                  w_exp_down[0], w_sh_gate[0], w_sh_up[0], w_sh_down[0], final_norm_g)
```
